```python
import math
import jax, jax.numpy as jnp
from jax import lax
import numpy as np

D_MODEL = 1024
BATCH = 8
SEQ = 2048
DEPTH = 4
DEC_BATCH = 128
DEC_SEQ = 8
PAST_LEN = 2048
PAGE_SIZE = 128

N_MIXERS = 2
N_ATTN_LAYERS = (DEPTH + N_MIXERS - 1) // N_MIXERS
N_GLA_LAYERS = DEPTH // N_MIXERS
ATTN_HEADS = 16
ATTN_HEAD_DIM = D_MODEL // ATTN_HEADS
MOBA_BLOCK = 256
MOBA_TOPK = 3
QUERY_BLOCK = 128
GLA_HEADS = 4
GLA_DK = D_MODEL // 2 // GLA_HEADS
GLA_DV = D_MODEL // GLA_HEADS
GLA_GATE_RANK = 16
GLA_GATE_TAU = 16.0
GLA_CHUNK = 64
GLA_IN_WIDTH = 2 * GLA_HEADS * GLA_DK + 2 * GLA_HEADS * GLA_DV + GLA_GATE_RANK
FFN_HIDDEN = ((8 * D_MODEL + 3 * 256 - 1) // (3 * 256)) * 256
NORM_EPS = 1e-6
NEG_INF = -1e30

kernel_name = 'moba_gla_hybrid_decode_step'


def rms_norm(x, g):
    xf = x.astype(jnp.float32)
    y = xf * lax.rsqrt(jnp.mean(xf * xf, axis=-1, keepdims=True) + NORM_EPS)
    return (y * g.astype(jnp.float32)).astype(x.dtype)


def swiglu_ffn(h, w_in, w_out):
    gate, up = jnp.split(h @ w_in, 2, axis=-1)
    return (jax.nn.silu(gate) * up) @ w_out


def moba_project(h, w_qkv, gq, gk):
    B, T, _ = h.shape
    q, k, v = jnp.split(h @ w_qkv, 3, axis=-1)
    shp = (B, T, ATTN_HEADS, ATTN_HEAD_DIM)
    return rms_norm(q.reshape(shp), gq), rms_norm(k.reshape(shp), gk), v.reshape(shp)


def moba_attend(q, k_own, v_own, own_mask, k_sel=None, v_sel=None, sel_valid=None):
    scale = ATTN_HEAD_DIM ** -0.5
    s_own = jnp.einsum('...hqd,...hld->...hql', q, k_own, preferred_element_type=jnp.float32) * scale
    s_own = jnp.where(own_mask, s_own, NEG_INF)
    if k_sel is None:
        p = jax.nn.softmax(s_own, axis=-1).astype(v_own.dtype)
        return jnp.einsum('...hql,...hld->...hqd', p, v_own)
    s_sel = jnp.einsum('...hqd,...hqld->...hql', q, k_sel, preferred_element_type=jnp.float32) * scale
    if sel_valid is not None:
        s_sel = jnp.where(sel_valid, s_sel, NEG_INF)
    n_sel = s_sel.shape[-1]
    p = jax.nn.softmax(jnp.concatenate([s_sel, s_own], axis=-1), axis=-1).astype(v_own.dtype)
    return (jnp.einsum('...hql,...hqld->...hqd', p[..., :n_sel], v_sel)
            + jnp.einsum('...hql,...hld->...hqd', p[..., n_sel:], v_own))


def moba_prompt(q, k, v):
    B, T, H, Dh = q.shape
    nb = -(-T // MOBA_BLOCK)
    pad = nb * MOBA_BLOCK - T

    def blocks(x):
        x = jnp.pad(x, ((0, 0), (0, pad), (0, 0), (0, 0)))
        return x.transpose(0, 2, 1, 3).reshape(B, H, nb, MOBA_BLOCK, Dh)

    kblk, vblk = blocks(k), blocks(v)
    kmean = jnp.mean(kblk.astype(jnp.float32), axis=3)
    qh = q.transpose(0, 2, 1, 3)
    nq = T // QUERY_BLOCK
    n_sel = min(MOBA_TOPK, nb - 1)
    head_idx = jnp.arange(H)[:, None, None]

    def one_block(bq):
        b, qi = bq[0], bq[1]
        q0 = qi * QUERY_BLOCK
        ob = q0 // MOBA_BLOCK
        q_blk = lax.dynamic_slice_in_dim(qh[b], q0, QUERY_BLOCK, axis=1)
        kb, vb = kblk[b], vblk[b]
        k_own = lax.dynamic_index_in_dim(kb, ob, axis=1, keepdims=False)
        v_own = lax.dynamic_index_in_dim(vb, ob, axis=1, keepdims=False)
        q_pos = q0 + jnp.arange(QUERY_BLOCK)
        k_pos = ob * MOBA_BLOCK + jnp.arange(MOBA_BLOCK)
        own_mask = k_pos[None, :] <= q_pos[:, None]
        if n_sel == 0:
            return moba_attend(q_blk, k_own, v_own, own_mask)
        gate = jnp.einsum('hqd,hnd->hqn', q_blk, kmean[b], preferred_element_type=jnp.float32)
        gate = jnp.where(jnp.arange(nb) < ob, gate, NEG_INF)
        _, idx = lax.top_k(gate, n_sel)
        valid = jnp.repeat(idx < ob, MOBA_BLOCK, axis=-1)
        k_sel = kb[head_idx, idx].reshape(H, QUERY_BLOCK, n_sel * MOBA_BLOCK, Dh)
        v_sel = vb[head_idx, idx].reshape(H, QUERY_BLOCK, n_sel * MOBA_BLOCK, Dh)
        return moba_attend(q_blk, k_own, v_own, own_mask, k_sel, v_sel, valid)

    pairs = jnp.stack([jnp.repeat(jnp.arange(B), nq), jnp.tile(jnp.arange(nq), B)], axis=1)
    out = lax.map(one_block, pairs)
    return out.reshape(B, nq, H, QUERY_BLOCK, Dh).transpose(0, 1, 3, 2, 4).reshape(B, T, H, Dh)


def moba_sample(q, k_new, v_new, ck, cv, page_table):
    DB, S, H, Dh = q.shape
    n_pages = page_table.shape[1]
    ppb = MOBA_BLOCK // PAGE_SIZE
    n_full = n_pages // ppb
    own_start = n_full * ppb
    r = (n_pages - own_start) * PAGE_SIZE
    pt_own = page_table[:, own_start:]
    k_own = jnp.concatenate([ck[pt_own].reshape(DB, r, H, Dh), k_new], axis=1).transpose(0, 2, 1, 3)
    v_own = jnp.concatenate([cv[pt_own].reshape(DB, r, H, Dh), v_new], axis=1).transpose(0, 2, 1, 3)
    own_mask = jnp.concatenate([jnp.ones((S, r), bool), jnp.tril(jnp.ones((S, S), bool))], axis=1)
    qh = q.transpose(0, 2, 1, 3)
    n_sel = min(MOBA_TOPK, n_full)
    if n_sel == 0:
        return moba_attend(qh, k_own, v_own, own_mask).transpose(0, 2, 1, 3)
    page_mean = jnp.mean(ck[page_table[:, :own_start]].astype(jnp.float32), axis=2)
    kmean = page_mean.reshape(DB, n_full, ppb, H, Dh).mean(axis=2)
    gate = jnp.einsum('bhsd,bnhd->bhsn', qh, kmean, preferred_element_type=jnp.float32)
    _, idx = lax.top_k(gate, n_sel)
    logical = idx[..., None] * ppb + jnp.arange(ppb)
    phys = page_table[jnp.arange(DB)[:, None, None, None, None], logical]
    head_idx = jnp.arange(H)[None, :, None, None]

    def one_query(args):
        q_i, phys_i, mask_i = args
        k_sel = ck[phys_i, :, head_idx].reshape(DB, H, 1, n_sel * MOBA_BLOCK, Dh)
        v_sel = cv[phys_i, :, head_idx].reshape(DB, H, 1, n_sel * MOBA_BLOCK, Dh)
        o = moba_attend(q_i[:, :, None], k_own, v_own, mask_i[None], k_sel, v_sel)
        return o[:, :, 0]

    out = lax.map(one_query, (qh.transpose(2, 0, 1, 3), phys.transpose(2, 0, 1, 3, 4), own_mask))
    return out.transpose(1, 0, 2, 3)


def gla_recurrence(q, k, v, log_a, s0):
    B, T, H, DK = q.shape
    DV = v.shape[-1]
    c = math.gcd(T, GLA_CHUNK)
    nc = T // c

    def chunks(x):
        return x.reshape(B, nc, c, H, x.shape[-1]).transpose(1, 0, 3, 2, 4)

    causal = jnp.tril(jnp.ones((c, c), bool))[:, :, None]

    def step(S, xs):
        qc, kc, vc, ac = xs
        b_cum = jnp.cumsum(ac, axis=2)
        diff = b_cum[:, :, :, None, :] - b_cum[:, :, None, :, :]
        decay = jnp.where(causal, jnp.exp(jnp.minimum(diff, 0.0)), 0.0)
        scores = jnp.einsum('bhtk,bhtsk,bhsk->bhts', qc, decay, kc)
        o = (jnp.einsum('bhts,bhsv->bhtv', scores, vc)
             + jnp.einsum('bhtk,bhkv->bhtv', qc * jnp.exp(b_cum), S))
        b_last = b_cum[:, :, -1]
        S_new = (jnp.exp(b_last)[..., None] * S
                 + jnp.einsum('bhsk,bhsv->bhkv', kc * jnp.exp(b_last[:, :, None] - b_cum), vc))
        return S_new, o

    S_final, o = lax.scan(step, s0, (chunks(q), chunks(k), chunks(v), chunks(log_a)))
    o = o.transpose(1, 0, 3, 2, 4).reshape(B, T, H, DV)
    return o, S_final


def gla_mixer(h, s0, w_in, w_gate, b_gate, g_out, w_o):
    B, T, _ = h.shape
    hk, hv = GLA_HEADS * GLA_DK, GLA_HEADS * GLA_DV
    proj = h @ w_in
    q = proj[..., :hk].reshape(B, T, GLA_HEADS, GLA_DK) * (GLA_DK ** -0.5)
    k = proj[..., hk:2 * hk].reshape(B, T, GLA_HEADS, GLA_DK)
    v = proj[..., 2 * hk:2 * hk + hv].reshape(B, T, GLA_HEADS, GLA_DV)
    r = proj[..., 2 * hk + hv:2 * hk + 2 * hv].reshape(B, T, GLA_HEADS, GLA_DV)
    g_low = proj[..., 2 * hk + 2 * hv:]
    log_a = jax.nn.log_sigmoid((g_low @ w_gate + b_gate).astype(jnp.float32)) / GLA_GATE_TAU
    log_a = log_a.reshape(B, T, GLA_HEADS, GLA_DK)
    o, s_new = gla_recurrence(q, k, v, log_a, s0.astype(jnp.float32))
    o = rms_norm(o, g_out).astype(h.dtype) * jax.nn.silu(r)
    return o.reshape(B, T, hv) @ w_o, s_new.astype(s0.dtype)


def setup_inputs(seed: int = 0) -> dict:
    key = jax.random.key(seed)
    ks = jax.random.split(key, 19)
    n_pages = PAST_LEN // PAGE_SIZE
    n_pool = (5 * DEC_BATCH * n_pages) // 4
    hd = ATTN_HEADS * ATTN_HEAD_DIM
    f32 = jnp.float32

    def normal(k, shape, scale=1.0):
        return jax.random.normal(k, shape, f32) * scale

    page_table = jax.random.permutation(ks[5], n_pool)[:DEC_BATCH * n_pages]
    page_table = page_table.reshape(DEC_BATCH, n_pages).astype(jnp.int32)
    return {
        'x_prompt': normal(ks[0], (BATCH, SEQ, D_MODEL)),
        'x_sample': normal(ks[1], (DEC_BATCH, DEC_SEQ, D_MODEL)),
        'cache_k': normal(ks[2], (N_ATTN_LAYERS, n_pool, PAGE_SIZE, ATTN_HEADS, ATTN_HEAD_DIM)),
        'cache_v': normal(ks[3], (N_ATTN_LAYERS, n_pool, PAGE_SIZE, ATTN_HEADS, ATTN_HEAD_DIM)),
        'state_gla': normal(ks[4], (N_GLA_LAYERS, DEC_BATCH, GLA_HEADS, GLA_DK, GLA_DV)),
        'page_table': page_table,
        'norm_mixer': 1.0 + normal(ks[6], (DEPTH, D_MODEL), 0.05),
        'norm_ffn': 1.0 + normal(ks[7], (DEPTH, D_MODEL), 0.05),
        'w_qkv': normal(ks[8], (N_ATTN_LAYERS, D_MODEL, 3 * hd), D_MODEL ** -0.5),
        'q_norm': 1.0 + normal(ks[9], (N_ATTN_LAYERS, ATTN_HEAD_DIM), 0.05),
        'k_norm': 1.0 + normal(ks[10], (N_ATTN_LAYERS, ATTN_HEAD_DIM), 0.05),
        'w_attn_o': normal(ks[11], (N_ATTN_LAYERS, hd, D_MODEL), hd ** -0.5),
        'w_gla_in': normal(ks[12], (N_GLA_LAYERS, D_MODEL, GLA_IN_WIDTH), D_MODEL ** -0.5),
        'w_gla_gate': normal(ks[13], (N_GLA_LAYERS, GLA_GATE_RANK, GLA_HEADS * GLA_DK), GLA_GATE_RANK ** -0.5),
        'b_gla_gate': normal(ks[14], (N_GLA_LAYERS, GLA_HEADS * GLA_DK), 0.1),
        'gla_norm': 1.0 + normal(ks[15], (N_GLA_LAYERS, GLA_DV), 0.05),
        'w_gla_o': normal(ks[16], (N_GLA_LAYERS, GLA_HEADS * GLA_DV, D_MODEL), (GLA_HEADS * GLA_DV) ** -0.5),
        'w_ffn_in': normal(ks[17], (DEPTH, D_MODEL, 2 * FFN_HIDDEN), D_MODEL ** -0.5),
        'w_ffn_out': normal(ks[18], (DEPTH, FFN_HIDDEN, D_MODEL), FFN_HIDDEN ** -0.5),
    }


def reference(x_prompt, x_sample, cache_k, cache_v, state_gla, page_table, norm_mixer, norm_ffn,
              w_qkv, q_norm, k_norm, w_attn_o, w_gla_in, w_gla_gate, b_gla_gate, gla_norm, w_gla_o,
              w_ffn_in, w_ffn_out):
    xp, xs = x_prompt, x_sample
    Bp, Tp, _ = xp.shape
    Bs, Ts, _ = xs.shape
    k_rows_p, v_rows_p, k_rows_s, v_rows_s, st_p, st_s = [], [], [], [], [], []
    for i in range(DEPTH):
        j = i // N_MIXERS
        hp = rms_norm(xp, norm_mixer[i])
        hs = rms_norm(xs, norm_mixer[i])
        if i % N_MIXERS == 0:
            qp, kp, vp = moba_project(hp, w_qkv[j], q_norm[j], k_norm[j])
            qs, ks_, vs = moba_project(hs, w_qkv[j], q_norm[j], k_norm[j])
            op = moba_prompt(qp, kp, vp)
            os_ = moba_sample(qs, ks_, vs, cache_k[j], cache_v[j], page_table)
            mp = op.reshape(Bp, Tp, -1) @ w_attn_o[j]
            ms = os_.reshape(Bs, Ts, -1) @ w_attn_o[j]
            k_rows_p.append(kp)
            v_rows_p.append(vp)
            k_rows_s.append(ks_)
            v_rows_s.append(vs)
        else:
            s0 = jnp.zeros((Bp, GLA_HEADS, GLA_DK, GLA_DV), xp.dtype)
            mp, sp = gla_mixer(hp, s0, w_gla_in[j], w_gla_gate[j], b_gla_gate[j], gla_norm[j], w_gla_o[j])
            ms, ss = gla_mixer(hs, state_gla[j], w_gla_in[j], w_gla_gate[j], b_gla_gate[j], gla_norm[j], w_gla_o[j])
            st_p.append(sp)
            st_s.append(ss)
        xp = xp + mp
        xs = xs + ms
        xp = xp + swiglu_ffn(rms_norm(xp, norm_ffn[i]), w_ffn_in[i], w_ffn_out[i])
        xs = xs + swiglu_ffn(rms_norm(xs, norm_ffn[i]), w_ffn_in[i], w_ffn_out[i])
    return (xp, xs, jnp.stack(k_rows_p), jnp.stack(v_rows_p), jnp.stack(k_rows_s), jnp.stack(v_rows_s),
            jnp.stack(st_p), jnp.stack(st_s))
```

```python
import functools

import jax
import jax.numpy as jnp
from jax import lax
from jax.experimental import pallas as pl
from jax.experimental.pallas import tpu as pltpu

F32, BF16 = jnp.float32, jnp.bfloat16

D_MODEL = 1024
DEPTH = 4
ATTN_HEADS = 16
HEAD_DIM = 64
MOBA_BLOCK = 256
MOBA_TOPK = 3
PAGE_SIZE = 128
GLA_HEADS = 4
GLA_DK = 128
GLA_DV = 256
GLA_GATE_RANK = 16
GLA_GATE_TAU = 16.0
GLA_CHUNK = 64
GLA_SUB = 16
FFN_HIDDEN = 2816
NORM_EPS = 1e-6
NEG_INF = -1e30

LANES = 128
SUBLANES = 8
VMEM_LIMIT_BYTES = 56 * 1024 * 1024

HEADS_PER_SLAB = LANES // HEAD_DIM
N_SLABS = ATTN_HEADS // HEADS_PER_SLAB

_NT = (((1,), (1,)), ((), ()))
_TN = (((0,), (0,)), ((), ()))


def _params(n_axes):
    return pltpu.CompilerParams(dimension_semantics=("arbitrary",) * n_axes,
                                vmem_limit_bytes=VMEM_LIMIT_BYTES)


def _rms_scale(x):
    return lax.rsqrt(jnp.mean(x * x, axis=-1, keepdims=True) + NORM_EPS)


def _silu(x):
    return x / (1.0 + jnp.exp(-x))


def _norm_matmul_body(x_ref, g_ref, w_ref, o_ref, h_ref):
    @pl.when(pl.program_id(1) == 0)
    def _():
        x = x_ref[...]
        h_ref[...] = (x * _rms_scale(x) * g_ref[...]).astype(BF16)

    o_ref[...] = jnp.dot(h_ref[...], w_ref[...], preferred_element_type=F32)


def norm_matmul(x, g, w, tm, tn, name):
    n, d = x.shape
    nc = w.shape[1]
    return pl.pallas_call(
        _norm_matmul_body,
        grid=(n // tm, nc // tn),
        in_specs=[pl.BlockSpec((tm, d), lambda i, j: (i, 0)),
                  pl.BlockSpec((1, d), lambda i, j: (0, 0)),
                  pl.BlockSpec((d, tn), lambda i, j: (0, j))],
        out_specs=pl.BlockSpec((tm, tn), lambda i, j: (i, j)),
        out_shape=jax.ShapeDtypeStruct((n, nc), F32),
        scratch_shapes=[pltpu.VMEM((tm, d), BF16)],
        compiler_params=_params(2),
        name=name,
    )(x, g, w)


def _post_ffn_body(x_ref, m_ref, wo_ref, g_ref, wg_ref, wu_ref, wd_ref, out_ref, x1_ref, h_ref, acc_ref):
    j = pl.program_id(1)

    @pl.when(j == 0)
    def _():
        x1 = x_ref[...] + jnp.dot(m_ref[...].astype(BF16), wo_ref[...], preferred_element_type=F32)
        x1_ref[...] = x1
        h_ref[...] = (x1 * _rms_scale(x1) * g_ref[...]).astype(BF16)
        acc_ref[...] = jnp.zeros_like(acc_ref)

    h = h_ref[...]
    gate = jnp.dot(h, wg_ref[...], preferred_element_type=F32)
    up = jnp.dot(h, wu_ref[...], preferred_element_type=F32)
    act = (_silu(gate) * up).astype(BF16)
    acc_ref[...] += jnp.dot(act, wd_ref[...], preferred_element_type=F32)

    @pl.when(j == pl.num_programs(1) - 1)
    def _():
        out_ref[...] = x1_ref[...] + acc_ref[...]


def post_ffn(x, m, wo, g, w_in, w_out, tm, th, name):
    n, d = x.shape
    nh = FFN_HIDDEN // th
    return pl.pallas_call(
        _post_ffn_body,
        grid=(n // tm, nh),
        in_specs=[pl.BlockSpec((tm, d), lambda i, j: (i, 0)),
                  pl.BlockSpec((tm, m.shape[1]), lambda i, j: (i, 0)),
                  pl.BlockSpec(wo.shape, lambda i, j: (0, 0)),
                  pl.BlockSpec((1, d), lambda i, j: (0, 0)),
                  pl.BlockSpec((d, th), lambda i, j: (0, j)),
                  pl.BlockSpec((d, th), lambda i, j: (0, j + nh)),
                  pl.BlockSpec((th, d), lambda i, j: (j, 0))],
        out_specs=pl.BlockSpec((tm, d), lambda i, j: (i, 0)),
        out_shape=jax.ShapeDtypeStruct((n, d), F32),
        scratch_shapes=[pltpu.VMEM((tm, d), F32), pltpu.VMEM((tm, d), BF16), pltpu.VMEM((tm, d), F32)],
        compiler_params=_params(2),
        name=name,
    )(x, m, wo, g, w_in, w_in, w_out)


def _head_pair_norm(y, gain, lo):
    y2 = y * y
    s_lo = jnp.sum(jnp.where(lo, y2, 0.0), axis=-1, keepdims=True)
    s_hi = jnp.sum(jnp.where(lo, 0.0, y2), axis=-1, keepdims=True)
    ms = jnp.where(lo, s_lo, s_hi) * (1.0 / HEAD_DIM)
    return y * lax.rsqrt(ms + NORM_EPS) * gain


def _top_blocks(gate):
    nb = gate.shape[1]
    col = lax.broadcasted_iota(jnp.int32, gate.shape, 1)
    rank = jnp.zeros(gate.shape, jnp.int32)
    for m in range(nb):
        c = gate[:, m:m + 1]
        beats = (c > gate) | ((c == gate) & (col > m))
        rank = rank + beats.astype(jnp.int32)
    return rank < MOBA_TOPK


def _moba_prompt_body(q_ref, k_ref, v_ref, gq_ref, gk_ref, o_ref, kn_ref, kb_ref, vb_ref, km_ref):
    i = pl.program_id(2)
    nb = k_ref.shape[0] // MOBA_BLOCK
    lo = lax.broadcasted_iota(jnp.int32, (1, LANES), 1) < HEAD_DIM

    @pl.when(i == 0)
    def _():
        kn = _head_pair_norm(k_ref[...], gk_ref[...], lo)
        kn_ref[...] = kn
        kb_ref[...] = kn.astype(BF16)
        vb_ref[...] = v_ref[...].astype(BF16)
        km_ref[...] = jnp.sum(kn.reshape(nb, MOBA_BLOCK, LANES), axis=1) * (1.0 / MOBA_BLOCK)

    qn = _head_pair_norm(q_ref[...], gq_ref[...], lo)
    qs = (qn * (HEAD_DIM ** -0.5)).astype(BF16)
    kmb = km_ref[...].astype(BF16)
    row = lax.broadcasted_iota(jnp.int32, (MOBA_BLOCK, MOBA_BLOCK), 0)
    col = lax.broadcasted_iota(jnp.int32, (MOBA_BLOCK, MOBA_BLOCK), 1)
    causal = col <= row
    blk = lax.broadcasted_iota(jnp.int32, (MOBA_BLOCK, nb), 1)
    past = blk < i

    own = pl.multiple_of(i * MOBA_BLOCK, MOBA_BLOCK)
    k_own = kb_ref[pl.ds(own, MOBA_BLOCK), :]
    v_own = vb_ref[pl.ds(own, MOBA_BLOCK), :]

    q_heads, sels, init = [], [], []
    for e in range(HEADS_PER_SLAB):
        mine = lo if e == 0 else jnp.logical_not(lo)
        qe = jnp.where(mine, qs, jnp.zeros_like(qs))
        gate = lax.dot_general(qe, kmb, _NT, preferred_element_type=F32)
        gate = jnp.where(past, gate, NEG_INF)
        sel = jnp.logical_and(_top_blocks(gate), past).astype(F32)
        s = lax.dot_general(qe, k_own, _NT, preferred_element_type=F32)
        s = jnp.where(causal, s, NEG_INF)
        m = jnp.max(s, axis=-1, keepdims=True)
        p = jnp.exp(s - m)
        l = jnp.sum(p, axis=-1, keepdims=True)
        acc = jnp.dot(p.astype(BF16), v_own, preferred_element_type=F32)
        q_heads.append(qe)
        sels.append(sel)
        init += [m, l, acc]

    def past_block(n, carry):
        off = pl.multiple_of(n * MOBA_BLOCK, MOBA_BLOCK)
        k_n = kb_ref[pl.ds(off, MOBA_BLOCK), :]
        v_n = vb_ref[pl.ds(off, MOBA_BLOCK), :]
        out = []
        for e in range(HEADS_PER_SLAB):
            m, l, acc = carry[3 * e:3 * e + 3]
            chosen = jnp.sum(jnp.where(blk == n, sels[e], 0.0), axis=-1, keepdims=True) > 0.5
            s = lax.dot_general(q_heads[e], k_n, _NT, preferred_element_type=F32)
            s = jnp.where(chosen, s, NEG_INF)
            m_new = jnp.maximum(m, jnp.max(s, axis=-1, keepdims=True))
            alpha = jnp.exp(m - m_new)
            p = jnp.exp(s - m_new)
            l = alpha * l + jnp.sum(p, axis=-1, keepdims=True)
            acc = alpha * acc + jnp.dot(p.astype(BF16), v_n, preferred_element_type=F32)
            out += [m_new, l, acc]
        return tuple(out)

    _, l0, a0, _, l1, a1 = lax.fori_loop(0, i, past_block, tuple(init))
    o_ref[...] = jnp.where(lo, a0 / l0, a1 / l1)


def moba_prompt(qkv, gq, gk, batch, seq, name):
    n = batch * seq
    nq = seq // MOBA_BLOCK
    return pl.pallas_call(
        _moba_prompt_body,
        grid=(batch, N_SLABS, nq),
        in_specs=[pl.BlockSpec((MOBA_BLOCK, LANES), lambda b, h, i: (b * nq + i, h)),
                  pl.BlockSpec((seq, LANES), lambda b, h, i: (b, N_SLABS + h)),
                  pl.BlockSpec((seq, LANES), lambda b, h, i: (b, 2 * N_SLABS + h)),
                  pl.BlockSpec((1, LANES), lambda b, h, i: (0, 0)),
                  pl.BlockSpec((1, LANES), lambda b, h, i: (0, 0))],
        out_specs=[pl.BlockSpec((MOBA_BLOCK, LANES), lambda b, h, i: (b * nq + i, h)),
                   pl.BlockSpec((seq, LANES), lambda b, h, i: (b, h))],
        out_shape=[jax.ShapeDtypeStruct((n, D_MODEL), F32), jax.ShapeDtypeStruct((n, D_MODEL), F32)],
        scratch_shapes=[pltpu.VMEM((seq, LANES), BF16), pltpu.VMEM((seq, LANES), BF16),
                        pltpu.VMEM((seq // MOBA_BLOCK, LANES), F32)],
        compiler_params=_params(3),
        name=name,
    )(qkv, qkv, qkv, gq, gk)


def _moba_sample_body(pt_ref, q_ref, k_ref, v_ref, gq_ref, gk_ref, hm_ref, hsel_ref, *rest, n_pages):
    del pt_ref
    k_pages, v_pages = rest[:n_pages], rest[n_pages:2 * n_pages]
    o_ref, kn_ref, s_ref = rest[2 * n_pages:]
    ppb = MOBA_BLOCK // PAGE_SIZE
    nb = n_pages // ppb
    s_new = q_ref.shape[0]
    rows = ATTN_HEADS * s_new
    lo = lax.broadcasted_iota(jnp.int32, (1, LANES), 1) < HEAD_DIM

    def heads_norm(y, gain):
        slabs = [_head_pair_norm(y[:, c * LANES:(c + 1) * LANES], gain[:, c * LANES:(c + 1) * LANES], lo)
                 for c in range(N_SLABS)]
        return jnp.concatenate(slabs, axis=1)

    qn = heads_norm(q_ref[...], gq_ref[...])
    kn = heads_norm(k_ref[...], gk_ref[...])
    kn_ref[...] = kn
    qs = qn * (HEAD_DIM ** -0.5)
    qbd = (jnp.broadcast_to(qs[None], (ATTN_HEADS, s_new, D_MODEL)).reshape(rows, D_MODEL) * hm_ref[...]).astype(BF16)

    kmeans = []
    for n in range(nb):
        kblk = jnp.concatenate([k_pages[n * ppb + p][...] for p in range(ppb)], axis=0)
        kmeans.append(jnp.sum(kblk, axis=0, keepdims=True) * (1.0 / MOBA_BLOCK))
        s_ref[:, n * MOBA_BLOCK:(n + 1) * MOBA_BLOCK] = lax.dot_general(
            qbd, kblk.astype(BF16), _NT, preferred_element_type=F32)
    kmean = jnp.concatenate(kmeans, axis=0)
    gate = lax.dot_general(qbd, kmean.astype(BF16), _NT, preferred_element_type=F32)
    sel = _top_blocks(gate)

    s_own = lax.dot_general(qbd, kn.astype(BF16), _NT, preferred_element_type=F32)
    q_pos = jnp.bitwise_and(lax.broadcasted_iota(jnp.int32, (rows, s_new), 0), s_new - 1)
    k_pos = lax.broadcasted_iota(jnp.int32, (rows, s_new), 1)
    s_own = jnp.where(k_pos <= q_pos, s_own, NEG_INF)

    def masked(n):
        return jnp.where(sel[:, n:n + 1], s_ref[:, n * MOBA_BLOCK:(n + 1) * MOBA_BLOCK], NEG_INF)

    m = jnp.max(s_own, axis=-1, keepdims=True)
    for n in range(nb):
        m = jnp.maximum(m, jnp.max(masked(n), axis=-1, keepdims=True))
    p_own = jnp.exp(s_own - m)
    l = jnp.sum(p_own, axis=-1, keepdims=True)
    acc = jnp.dot(p_own.astype(BF16), v_ref[...].astype(BF16), preferred_element_type=F32)
    for n in range(nb):
        p = jnp.exp(masked(n) - m)
        l = l + jnp.sum(p, axis=-1, keepdims=True)
        vblk = jnp.concatenate([v_pages[n * ppb + q][...] for q in range(ppb)], axis=0).astype(BF16)
        acc = acc + jnp.dot(p.astype(BF16), vblk, preferred_element_type=F32)
    o = acc / l
    o_ref[...] = jnp.sum(o.reshape(ATTN_HEADS, s_new, D_MODEL) * hsel_ref[...][:, None, :], axis=0)


def moba_sample(qkv, gq, gk, cache_k, cache_v, layer, page_table, name):
    dec_batch, n_pages = page_table.shape
    n = qkv.shape[0]
    s_new = n // dec_batch
    assert s_new & (s_new - 1) == 0, "new-token count must be a power of two"
    rows = ATTN_HEADS * s_new
    head_of_lane = jnp.arange(D_MODEL) // HEAD_DIM
    hm = (jnp.arange(rows)[:, None] // s_new == head_of_lane[None, :]).astype(F32)
    hsel = (jnp.arange(ATTN_HEADS)[:, None] == head_of_lane[None, :]).astype(F32)

    def page_spec(p):
        return pl.BlockSpec((None, None, PAGE_SIZE, D_MODEL),
                            lambda b, pt: (layer, pt[b * n_pages + p], 0, 0))

    const = lambda b, pt: (0, 0)
    grid_spec = pltpu.PrefetchScalarGridSpec(
        num_scalar_prefetch=1,
        grid=(dec_batch,),
        in_specs=[pl.BlockSpec((s_new, D_MODEL), lambda b, pt: (b, 0)),
                  pl.BlockSpec((s_new, D_MODEL), lambda b, pt: (b, 1)),
                  pl.BlockSpec((s_new, D_MODEL), lambda b, pt: (b, 2)),
                  pl.BlockSpec((1, D_MODEL), const),
                  pl.BlockSpec((1, D_MODEL), const),
                  pl.BlockSpec((rows, D_MODEL), const),
                  pl.BlockSpec((ATTN_HEADS, D_MODEL), const)]
                 + [page_spec(p) for p in range(n_pages)] * 2,
        out_specs=[pl.BlockSpec((s_new, D_MODEL), lambda b, pt: (b, 0)),
                   pl.BlockSpec((s_new, D_MODEL), lambda b, pt: (b, 0))],
        scratch_shapes=[pltpu.VMEM((rows, n_pages * PAGE_SIZE), F32)],
    )
    return pl.pallas_call(
        functools.partial(_moba_sample_body, n_pages=n_pages),
        grid_spec=grid_spec,
        out_shape=[jax.ShapeDtypeStruct((n, D_MODEL), F32), jax.ShapeDtypeStruct((n, D_MODEL), F32)],
        compiler_params=_params(1),
        name=name,
    )(page_table.reshape(-1), qkv, qkv, qkv, gq, gk, hm, hsel,
      *([cache_k] * n_pages), *([cache_v] * n_pages))


def _cumsum_rows(tri, a):
    hi = a.astype(BF16)
    r1 = a - hi.astype(F32)
    mid = r1.astype(BF16)
    lo = (r1 - mid.astype(F32)).astype(BF16)
    n = a.shape[1]
    r = jnp.dot(tri, jnp.concatenate([hi, mid, lo], axis=1), preferred_element_type=F32)
    return r[:, :n] + r[:, n:2 * n] + r[:, 2 * n:]


def _gla_intra(q, k, b, c, u):
    row_c = lax.broadcasted_iota(jnp.int32, (c, 1), 0)
    row_u = lax.broadcasted_iota(jnp.int32, (u, 1), 0)
    col = lax.broadcasted_iota(jnp.int32, (u, c), 1)
    parts = []
    for i in range(c // u):
        r0 = i * u
        qi, bi = q[r0:r0 + u], b[r0:r0 + u]
        if i == 0:
            a_i = jnp.zeros((u, c), F32)
        else:
            bref = b[r0 - 1:r0]
            qt = (qi * jnp.exp(bi - bref)).astype(BF16)
            kt = jnp.where(row_c < r0, k * jnp.exp(jnp.minimum(bref - b, 0.0)), 0.0).astype(BF16)
            a_i = lax.dot_general(qt, kt, _NT, preferred_element_type=F32)
        for s in range(u):
            ks, bs = k[r0 + s:r0 + s + 1], b[r0 + s:r0 + s + 1]
            w = jnp.sum(qi * ks * jnp.exp(jnp.minimum(bi - bs, 0.0)), axis=-1, keepdims=True)
            a_i = jnp.where((col == r0 + s) & (row_u >= s), w, a_i)
        parts.append(a_i)
    return parts[0] if len(parts) == 1 else jnp.concatenate(parts, axis=0)


def _gla_body(q_ref, k_ref, v_ref, r_ref, gl_ref, wg_ref, bg_ref, tri_ref, gn_ref, s0_ref,
              o_ref, so_ref, s_ref, *, c, u, n_chunks):
    t = pl.program_id(1)

    @pl.when(t == 0)
    def _():
        s_ref[...] = s0_ref[...]

    tri = tri_ref[...]
    scale = GLA_DK ** -0.5

    def chunk(ci, carry):
        rows = pl.ds(pl.multiple_of(ci * c, c), c)
        x = jnp.dot(gl_ref[rows, :].astype(BF16), wg_ref[...], preferred_element_type=F32) + bg_ref[...]
        log_a = (jnp.minimum(x, 0.0) - jnp.log1p(jnp.exp(-jnp.abs(x)))) * (1.0 / GLA_GATE_TAU)
        b_all = _cumsum_rows(tri, log_a)
        for hd in range(GLA_HEADS):
            kc = slice(hd * GLA_DK, (hd + 1) * GLA_DK)
            vc = slice(hd * GLA_DV, (hd + 1) * GLA_DV)
            q = q_ref[rows, kc] * scale
            k = k_ref[rows, kc]
            b = b_all[:, kc]
            v = v_ref[rows, vc].astype(BF16)
            state = s_ref[hd]
            o = jnp.dot((q * jnp.exp(b)).astype(BF16), state.astype(BF16), preferred_element_type=F32)
            a = _gla_intra(q, k, b, c, u)
            o = o + jnp.dot(a.astype(BF16), v, preferred_element_type=F32)
            b_last = b[c - 1:c]
            kt = (k * jnp.exp(b_last - b)).astype(BF16)
            decay = jnp.broadcast_to(jnp.exp(b_last), (SUBLANES, GLA_DK)).T[:, :1]
            s_ref[hd] = state * decay + lax.dot_general(kt, v, _TN, preferred_element_type=F32)
            rr = r_ref[rows, vc]
            o_ref[rows, vc] = o * _rms_scale(o) * gn_ref[:, vc] * _silu(rr)
        return carry

    lax.fori_loop(0, n_chunks, chunk, 0)

    @pl.when(t == pl.num_programs(1) - 1)
    def _():
        so_ref[...] = s_ref[...]


def gla_mix(proj, w_gate, b_gate, g_out, s0, layer, batch, seq, c, u, tt, name):
    n = batch * seq
    nt = seq // tt
    hk, hv = GLA_HEADS * GLA_DK, GLA_HEADS * GLA_DV
    tri = jnp.tril(jnp.ones((c, c), F32)).astype(BF16)
    row = lambda b, t: b * nt + t
    state_block = (None, None, GLA_HEADS, GLA_DK, GLA_DV)
    return pl.pallas_call(
        functools.partial(_gla_body, c=c, u=u, n_chunks=tt // c),
        grid=(batch, nt),
        in_specs=[pl.BlockSpec((tt, hk), lambda b, t: (row(b, t), 0)),
                  pl.BlockSpec((tt, hk), lambda b, t: (row(b, t), 1)),
                  pl.BlockSpec((tt, hv), lambda b, t: (row(b, t), 1)),
                  pl.BlockSpec((tt, hv), lambda b, t: (row(b, t), 2)),
                  pl.BlockSpec((tt, LANES), lambda b, t: (row(b, t), (2 * hk + 2 * hv) // LANES)),
                  pl.BlockSpec(w_gate.shape, lambda b, t: (0, 0)),
                  pl.BlockSpec((1, hk), lambda b, t: (0, 0)),
                  pl.BlockSpec((c, c), lambda b, t: (0, 0)),
                  pl.BlockSpec((1, hv), lambda b, t: (0, 0)),
                  pl.BlockSpec(state_block, lambda b, t: (layer, b, 0, 0, 0))],
        out_specs=[pl.BlockSpec((tt, hv), lambda b, t: (row(b, t), 0)),
                   pl.BlockSpec((None, GLA_HEADS, GLA_DK, GLA_DV), lambda b, t: (b, 0, 0, 0))],
        out_shape=[jax.ShapeDtypeStruct((n, hv), F32),
                   jax.ShapeDtypeStruct((batch, GLA_HEADS, GLA_DK, GLA_DV), F32)],
        scratch_shapes=[pltpu.VMEM((GLA_HEADS, GLA_DK, GLA_DV), F32)],
        compiler_params=_params(2),
        name=name,
    )(proj, proj, proj, proj, proj, w_gate, b_gate, tri, g_out, s0)


def kernel(x_prompt, x_sample, cache_k, cache_v, state_gla, page_table, norm_mixer, norm_ffn, w_qkv, q_norm,
           k_norm, w_attn_o, w_gla_in, w_gla_gate, b_gla_gate, gla_norm, w_gla_o, w_ffn_in, w_ffn_out):
    bp, tp, d = x_prompt.shape
    bs, ts, _ = x_sample.shape
    xp = x_prompt.reshape(bp * tp, d)
    xs = x_sample.reshape(bs * ts, d)
    n_layers_attn, n_pool = cache_k.shape[:2]
    ck = cache_k.reshape(n_layers_attn, n_pool, PAGE_SIZE, d)
    cv = cache_v.reshape(n_layers_attn, n_pool, PAGE_SIZE, d)
    hk, hv = GLA_HEADS * GLA_DK, GLA_HEADS * GLA_DV
    zero_state = jnp.zeros((1, bp, GLA_HEADS, GLA_DK, GLA_DV), F32)
    tm_p, tm_s = 512, 512
    head_shape = (ATTN_HEADS, HEAD_DIM)

    k_rows_p, v_rows_p, k_rows_s, v_rows_s, st_p, st_s = [], [], [], [], [], []
    for i in range(DEPTH):
        j = i // 2
        g_mix = norm_mixer[i][None]
        if i % 2 == 0:
            w = w_qkv[j].astype(BF16)
            qkv_p = norm_matmul(xp, g_mix, w, tm_p, 1024, f"qkv_p{i}")
            qkv_s = norm_matmul(xs, g_mix, w, tm_s, 1024, f"qkv_s{i}")
            gq = jnp.tile(q_norm[j], ATTN_HEADS)[None]
            gk = jnp.tile(k_norm[j], ATTN_HEADS)[None]
            mp, kn_p = moba_prompt(qkv_p, gq[:, :LANES], gk[:, :LANES], bp, tp, f"moba_p{i}")
            ms, kn_s = moba_sample(qkv_s, gq, gk, ck, cv, j, page_table, f"moba_s{i}")
            k_rows_p.append(kn_p.reshape(bp, tp, *head_shape))
            v_rows_p.append(qkv_p[:, 2 * d:].reshape(bp, tp, *head_shape))
            k_rows_s.append(kn_s.reshape(bs, ts, *head_shape))
            v_rows_s.append(qkv_s[:, 2 * d:].reshape(bs, ts, *head_shape))
            wo = w_attn_o[j].astype(BF16)
        else:
            w_main = w_gla_in[j][:, :2 * hk + 2 * hv]
            w_low = jnp.pad(w_gla_in[j][:, 2 * hk + 2 * hv:], ((0, 0), (0, LANES - GLA_GATE_RANK)))
            w = jnp.concatenate([w_main, w_low], axis=1).astype(BF16)
            proj_p = norm_matmul(xp, g_mix, w, tm_p, 640, f"gla_in_p{i}")
            proj_s = norm_matmul(xs, g_mix, w, tm_s, 640, f"gla_in_s{i}")
            w_gate = jnp.pad(w_gla_gate[j], ((0, LANES - GLA_GATE_RANK), (0, 0))).astype(BF16)
            b_gate = b_gla_gate[j][None]
            g_out = jnp.tile(gla_norm[j], GLA_HEADS)[None]
            mp, sp = gla_mix(proj_p, w_gate, b_gate, g_out, zero_state, 0, bp, tp,
                             GLA_CHUNK, GLA_SUB, 256, f"gla_p{i}")
            ms, ss = gla_mix(proj_s, w_gate, b_gate, g_out, state_gla, j, bs, ts, ts, ts, ts, f"gla_s{i}")
            st_p.append(sp)
            st_s.append(ss)
            wo = w_gla_o[j].astype(BF16)
        g_ffn = norm_ffn[i][None]
        w_in = w_ffn_in[i].astype(BF16)
        w_out = w_ffn_out[i].astype(BF16)
        xp = post_ffn(xp, mp, wo, g_ffn, w_in, w_out, tm_p, 256, f"ffn_p{i}")
        xs = post_ffn(xs, ms, wo, g_ffn, w_in, w_out, tm_s, 256, f"ffn_s{i}")
    return (xp.reshape(bp, tp, d), xs.reshape(bs, ts, d), jnp.stack(k_rows_p), jnp.stack(v_rows_p),
            jnp.stack(k_rows_s), jnp.stack(v_rows_s), jnp.stack(st_p), jnp.stack(st_s))
```

```python
import functools

import jax
import jax.numpy as jnp
from jax import lax
from jax.experimental import pallas as pl
from jax.experimental.pallas import tpu as pltpu

F32, BF16 = jnp.float32, jnp.bfloat16

D_MODEL = 1024
DEPTH = 4
ATTN_HEADS = 16
HEAD_DIM = 64
MOBA_BLOCK = 256
MOBA_TOPK = 3
PAGE_SIZE = 128
GLA_HEADS = 4
GLA_DK = 128
GLA_DV = 256
GLA_GATE_RANK = 16
GLA_GATE_TAU = 16.0
GLA_CHUNK = 64
GLA_SUB = 16
FFN_HIDDEN = 2816
NORM_EPS = 1e-6
NEG_INF = -1e30

LANES = 128
SUBLANES = 8
VMEM_LIMIT_BYTES = 56 * 1024 * 1024

HEADS_PER_SLAB = LANES // HEAD_DIM
N_SLABS = ATTN_HEADS // HEADS_PER_SLAB

_NT = (((1,), (1,)), ((), ()))
_TN = (((0,), (0,)), ((), ()))


def _params(n_axes):
    return pltpu.CompilerParams(dimension_semantics=("arbitrary",) * n_axes,
                                vmem_limit_bytes=VMEM_LIMIT_BYTES)


def _rms_scale(x):
    return lax.rsqrt(jnp.mean(x * x, axis=-1, keepdims=True) + NORM_EPS)


def _silu(x):
    return x / (1.0 + jnp.exp(-x))


def _norm_matmul_body(x_ref, g_ref, w_ref, *rest):
    o_refs, h_ref = rest[:-1], rest[-1]
    j = pl.program_id(1)

    @pl.when(j == 0)
    def _():
        x = x_ref[...]
        h_ref[...] = (x * _rms_scale(x) * g_ref[...]).astype(BF16)

    y = jnp.dot(h_ref[...], w_ref[...], preferred_element_type=F32)
    if len(o_refs) == 1:
        o_refs[0][...] = y
    else:
        for c, o_ref in enumerate(o_refs):
            @pl.when(j == c)
            def _(o_ref=o_ref):
                o_ref[...] = y


def norm_matmul(x, g, w, tm, tn, name, split=False):
    n, d = x.shape
    nc = w.shape[1]
    if split:
        out_specs = [pl.BlockSpec((tm, tn), lambda i, j: (i, 0))] * (nc // tn)
        out_shape = [jax.ShapeDtypeStruct((n, tn), F32)] * (nc // tn)
    else:
        out_specs = pl.BlockSpec((tm, tn), lambda i, j: (i, j))
        out_shape = jax.ShapeDtypeStruct((n, nc), F32)
    return pl.pallas_call(
        _norm_matmul_body,
        grid=(n // tm, nc // tn),
        in_specs=[pl.BlockSpec((tm, d), lambda i, j: (i, 0)),
                  pl.BlockSpec((1, d), lambda i, j: (0, 0)),
                  pl.BlockSpec((d, tn), lambda i, j: (0, j))],
        out_specs=out_specs,
        out_shape=out_shape,
        scratch_shapes=[pltpu.VMEM((tm, d), BF16)],
        compiler_params=_params(2),
        name=name,
    )(x, g, w)


def _post_ffn_body(x_ref, m_ref, wo_ref, g_ref, wg_ref, wu_ref, wd_ref, out_ref, x1_ref, h_ref, acc_ref):
    j = pl.program_id(1)

    @pl.when(j == 0)
    def _():
        x1 = x_ref[...] + jnp.dot(m_ref[...].astype(BF16), wo_ref[...], preferred_element_type=F32)
        x1_ref[...] = x1
        h_ref[...] = (x1 * _rms_scale(x1) * g_ref[...]).astype(BF16)
        acc_ref[...] = jnp.zeros_like(acc_ref)

    h = h_ref[...]
    gate = jnp.dot(h, wg_ref[...], preferred_element_type=F32)
    up = jnp.dot(h, wu_ref[...], preferred_element_type=F32)
    act = (_silu(gate) * up).astype(BF16)
    acc_ref[...] += jnp.dot(act, wd_ref[...], preferred_element_type=F32)

    @pl.when(j == pl.num_programs(1) - 1)
    def _():
        out_ref[...] = x1_ref[...] + acc_ref[...]


def post_ffn(x, m, wo, g, w_in, w_out, tm, th, name):
    n, d = x.shape
    nh = FFN_HIDDEN // th
    return pl.pallas_call(
        _post_ffn_body,
        grid=(n // tm, nh),
        in_specs=[pl.BlockSpec((tm, d), lambda i, j: (i, 0)),
                  pl.BlockSpec((tm, m.shape[1]), lambda i, j: (i, 0)),
                  pl.BlockSpec(wo.shape, lambda i, j: (0, 0)),
                  pl.BlockSpec((1, d), lambda i, j: (0, 0)),
                  pl.BlockSpec((d, th), lambda i, j: (0, j)),
                  pl.BlockSpec((d, th), lambda i, j: (0, j + nh)),
                  pl.BlockSpec((th, d), lambda i, j: (j, 0))],
        out_specs=pl.BlockSpec((tm, d), lambda i, j: (i, 0)),
        out_shape=jax.ShapeDtypeStruct((n, d), F32),
        scratch_shapes=[pltpu.VMEM((tm, d), F32), pltpu.VMEM((tm, d), BF16), pltpu.VMEM((tm, d), F32)],
        compiler_params=_params(2),
        name=name,
    )(x, m, wo, g, w_in, w_in, w_out)


def _head_pair_norm(y, gain, lo):
    y2 = y * y
    s_lo = jnp.sum(jnp.where(lo, y2, 0.0), axis=-1, keepdims=True)
    s_hi = jnp.sum(jnp.where(lo, 0.0, y2), axis=-1, keepdims=True)
    ms = jnp.where(lo, s_lo, s_hi) * (1.0 / HEAD_DIM)
    return y * lax.rsqrt(ms + NORM_EPS) * gain


def _top_blocks(gate, axis):
    nb = gate.shape[axis]
    idx = lax.broadcasted_iota(jnp.int32, gate.shape, axis)
    rank = jnp.zeros(gate.shape, jnp.int32)
    for m in range(nb):
        c = lax.slice_in_dim(gate, m, m + 1, axis=axis)
        beats = (c > gate) | ((c == gate) & (idx > m))
        rank = rank + beats.astype(jnp.int32)
    return rank < MOBA_TOPK


def _moba_prompt_body(q_ref, k_ref, v_ref, gq_ref, gk_ref, o_ref, kn_ref, kb_ref, vt_ref, km_ref, s_ref):
    i = pl.program_id(2)
    nb = k_ref.shape[0] // MOBA_BLOCK
    tq = q_ref.shape[0]
    lo = lax.broadcasted_iota(jnp.int32, (1, LANES), 1) < HEAD_DIM

    @pl.when(i == 0)
    def _():
        kn = _head_pair_norm(k_ref[...], gk_ref[...], lo)
        kn_ref[...] = kn
        km_ref[...] = jnp.sum(kn.reshape(nb, MOBA_BLOCK, LANES), axis=1) * (1.0 / MOBA_BLOCK)
        for n in range(nb):
            rows = slice(n * MOBA_BLOCK, (n + 1) * MOBA_BLOCK)
            kb_ref[n] = kn[rows].astype(BF16)
            vt_ref[n] = v_ref[rows, :].T.astype(BF16)

    qn = _head_pair_norm(q_ref[...], gq_ref[...], lo)
    qt = (qn * (HEAD_DIM ** -0.5)).T.astype(BF16)
    kmb = km_ref[...].astype(BF16)
    dim = lax.broadcasted_iota(jnp.int32, (LANES, 1), 0)
    zero = jnp.zeros_like(qt)
    qcat = jnp.concatenate([jnp.where(dim < HEAD_DIM, qt, zero), jnp.where(dim >= HEAD_DIM, qt, zero)], axis=1)
    blk = lax.broadcasted_iota(jnp.int32, (nb, 2 * tq), 0)
    past = blk < i
    gate = jnp.dot(kmb, qcat, preferred_element_type=F32)
    gate = jnp.where(past, gate, NEG_INF)
    sel = jnp.logical_and(_top_blocks(gate, 0), past).astype(F32)

    key = lax.broadcasted_iota(jnp.int32, (MOBA_BLOCK, 2 * tq), 0)
    qry = jnp.bitwise_and(lax.broadcasted_iota(jnp.int32, (MOBA_BLOCK, 2 * tq), 1), tq - 1)
    s_own = jnp.dot(kb_ref[i], qcat, preferred_element_type=F32)
    s_own = jnp.where(key <= qry, s_own, NEG_INF)
    m_own = jnp.max(s_own, axis=0, keepdims=True)

    def attend(n_blocks):
        m = m_own
        for n in range(n_blocks):
            rows = slice(n * MOBA_BLOCK, (n + 1) * MOBA_BLOCK)
            s = jnp.dot(kb_ref[n], qcat, preferred_element_type=F32)
            s = jnp.where(sel[n:n + 1, :] > 0.5, s, NEG_INF)
            s_ref[rows, :] = s
            m = jnp.maximum(m, jnp.max(s, axis=0, keepdims=True))
        p = jnp.exp(s_own - m)
        l = jnp.sum(p, axis=0, keepdims=True)
        acc = jnp.dot(vt_ref[i], p.astype(BF16), preferred_element_type=F32)
        for n in range(n_blocks):
            rows = slice(n * MOBA_BLOCK, (n + 1) * MOBA_BLOCK)
            p = jnp.exp(s_ref[rows, :] - m)
            l = l + jnp.sum(p, axis=0, keepdims=True)
            acc = acc + jnp.dot(vt_ref[n], p.astype(BF16), preferred_element_type=F32)
        o = acc / l
        o_ref[...] = jnp.concatenate([o[:HEAD_DIM, :tq], o[HEAD_DIM:, tq:]], axis=0).T

    variants = sorted(set(range(1, nb, 2)) | {nb - 1})
    for idx, n_blocks in enumerate(variants):
        first = variants[idx - 1] + 1 if idx else 0
        pl.when(jnp.logical_and(i >= first, i <= n_blocks))(functools.partial(attend, n_blocks))


def moba_prompt(q, k, v, gq, gk, batch, seq, name):
    n = batch * seq
    nb = seq // MOBA_BLOCK
    return pl.pallas_call(
        _moba_prompt_body,
        grid=(batch, N_SLABS, nb),
        in_specs=[pl.BlockSpec((MOBA_BLOCK, LANES), lambda b, h, i: (b * nb + i, h)),
                  pl.BlockSpec((seq, LANES), lambda b, h, i: (b, h)),
                  pl.BlockSpec((seq, LANES), lambda b, h, i: (b, h)),
                  pl.BlockSpec((1, LANES), lambda b, h, i: (0, 0)),
                  pl.BlockSpec((1, LANES), lambda b, h, i: (0, 0))],
        out_specs=[pl.BlockSpec((MOBA_BLOCK, LANES), lambda b, h, i: (b * nb + i, h)),
                   pl.BlockSpec((seq, LANES), lambda b, h, i: (b, h))],
        out_shape=[jax.ShapeDtypeStruct((n, D_MODEL), F32), jax.ShapeDtypeStruct((n, D_MODEL), F32)],
        scratch_shapes=[pltpu.VMEM((nb, MOBA_BLOCK, LANES), BF16), pltpu.VMEM((nb, LANES, MOBA_BLOCK), BF16),
                        pltpu.VMEM((nb, LANES), F32),
                        pltpu.VMEM(((nb - 1) * MOBA_BLOCK, HEADS_PER_SLAB * MOBA_BLOCK), F32)],
        compiler_params=_params(3),
        name=name,
    )(q, k, v, gq, gk)


def _moba_sample_body(pt_ref, q_ref, k_ref, v_ref, gq_ref, gk_ref, hm_ref, hsel_ref, *rest, n_pages):
    del pt_ref
    k_pages, v_pages = rest[:n_pages], rest[n_pages:2 * n_pages]
    o_ref, kn_ref, s_ref = rest[2 * n_pages:]
    ppb = MOBA_BLOCK // PAGE_SIZE
    nb = n_pages // ppb
    s_new = q_ref.shape[0]
    rows = ATTN_HEADS * s_new
    lo = lax.broadcasted_iota(jnp.int32, (1, LANES), 1) < HEAD_DIM

    def heads_norm(y, gain):
        slabs = [_head_pair_norm(y[:, c * LANES:(c + 1) * LANES], gain[:, c * LANES:(c + 1) * LANES], lo)
                 for c in range(N_SLABS)]
        return jnp.concatenate(slabs, axis=1)

    qn = heads_norm(q_ref[...], gq_ref[...])
    kn = heads_norm(k_ref[...], gk_ref[...])
    kn_ref[...] = kn
    qs = qn * (HEAD_DIM ** -0.5)
    qbd = (jnp.broadcast_to(qs[None], (ATTN_HEADS, s_new, D_MODEL)).reshape(rows, D_MODEL) * hm_ref[...]).astype(BF16)

    for p in range(n_pages):
        s_ref[:, p * PAGE_SIZE:(p + 1) * PAGE_SIZE] = jnp.dot(
            qbd, k_pages[p][...].astype(BF16), preferred_element_type=F32)

    def block_scores(n):
        return s_ref[:, n * MOBA_BLOCK:(n + 1) * MOBA_BLOCK]

    blk = lax.broadcasted_iota(jnp.int32, (rows, nb), 1)
    gate = jnp.zeros((rows, nb), F32)
    for n in range(nb):
        g_n = jnp.sum(block_scores(n), axis=-1, keepdims=True) * (1.0 / MOBA_BLOCK)
        gate = jnp.where(blk == n, g_n, gate)
    sel = _top_blocks(gate, 1)

    s_own = lax.dot_general(qbd, kn.astype(BF16), _NT, preferred_element_type=F32)
    q_pos = jnp.bitwise_and(lax.broadcasted_iota(jnp.int32, (rows, s_new), 0), s_new - 1)
    k_pos = lax.broadcasted_iota(jnp.int32, (rows, s_new), 1)
    s_own = jnp.where(k_pos <= q_pos, s_own, NEG_INF)

    def masked(n):
        return jnp.where(sel[:, n:n + 1], block_scores(n), NEG_INF)

    m = jnp.max(s_own, axis=-1, keepdims=True)
    for n in range(nb):
        m = jnp.maximum(m, jnp.max(masked(n), axis=-1, keepdims=True))
    p_own = jnp.exp(s_own - m)
    l = jnp.sum(p_own, axis=-1, keepdims=True)
    acc = jnp.dot(p_own.astype(BF16), v_ref[...].astype(BF16), preferred_element_type=F32)
    for n in range(nb):
        p = jnp.exp(masked(n) - m)
        l = l + jnp.sum(p, axis=-1, keepdims=True)
        pb = p.astype(BF16)
        for c in range(ppb):
            acc = acc + lax.dot_general(pb[:, c * PAGE_SIZE:(c + 1) * PAGE_SIZE],
                                        v_pages[n * ppb + c][...].astype(BF16), _NT, preferred_element_type=F32)
    o = acc / l
    o_ref[...] = jnp.sum(o.reshape(ATTN_HEADS, s_new, D_MODEL) * hsel_ref[...][:, None, :], axis=0)


def moba_sample(q, k, v, gq, gk, cache_kt, cache_vt, layer, page_table, name):
    dec_batch, n_pages = page_table.shape
    n = q.shape[0]
    s_new = n // dec_batch
    assert s_new & (s_new - 1) == 0, "new-token count must be a power of two"
    rows = ATTN_HEADS * s_new
    head_of_lane = jnp.arange(D_MODEL) // HEAD_DIM
    hm = (jnp.arange(rows)[:, None] // s_new == head_of_lane[None, :]).astype(F32)
    hsel = (jnp.arange(ATTN_HEADS)[:, None] == head_of_lane[None, :]).astype(F32)

    def page_spec(p):
        return pl.BlockSpec((None, None, D_MODEL, PAGE_SIZE),
                            lambda b, pt: (layer, pt[b * n_pages + p], 0, 0))

    const = lambda b, pt: (0, 0)
    token = pl.BlockSpec((s_new, D_MODEL), lambda b, pt: (b, 0))
    grid_spec = pltpu.PrefetchScalarGridSpec(
        num_scalar_prefetch=1,
        grid=(dec_batch,),
        in_specs=[token, token, token,
                  pl.BlockSpec((1, D_MODEL), const),
                  pl.BlockSpec((1, D_MODEL), const),
                  pl.BlockSpec((rows, D_MODEL), const),
                  pl.BlockSpec((ATTN_HEADS, D_MODEL), const)]
                 + [page_spec(p) for p in range(n_pages)] * 2,
        out_specs=[token, token],
        scratch_shapes=[pltpu.VMEM((rows, n_pages * PAGE_SIZE), F32)],
    )
    return pl.pallas_call(
        functools.partial(_moba_sample_body, n_pages=n_pages),
        grid_spec=grid_spec,
        out_shape=[jax.ShapeDtypeStruct((n, D_MODEL), F32), jax.ShapeDtypeStruct((n, D_MODEL), F32)],
        compiler_params=_params(1),
        name=name,
    )(page_table.reshape(-1), q, k, v, gq, gk, hm, hsel,
      *([cache_kt] * n_pages), *([cache_vt] * n_pages))


def _cumsum_rows(tri, a):
    hi = a.astype(BF16)
    r1 = a - hi.astype(F32)
    mid = r1.astype(BF16)
    lo = (r1 - mid.astype(F32)).astype(BF16)
    n = a.shape[1]
    r = jnp.dot(tri, jnp.concatenate([hi, mid, lo], axis=1), preferred_element_type=F32)
    return r[:, :n] + r[:, n:2 * n] + r[:, 2 * n:]


def _gla_intra(q, k, b, c, u):
    row_c = lax.broadcasted_iota(jnp.int32, (c, 1), 0)
    row_u = lax.broadcasted_iota(jnp.int32, (u, 1), 0)
    col = lax.broadcasted_iota(jnp.int32, (u, c), 1)
    parts = []
    for i in range(c // u):
        r0 = i * u
        qi, bi = q[r0:r0 + u], b[r0:r0 + u]
        if i == 0:
            a_i = jnp.zeros((u, c), F32)
        else:
            bref = b[r0 - 1:r0]
            qt = (qi * jnp.exp(bi - bref)).astype(BF16)
            kt = jnp.where(row_c < r0, k * jnp.exp(jnp.minimum(bref - b, 0.0)), 0.0).astype(BF16)
            a_i = lax.dot_general(qt, kt, _NT, preferred_element_type=F32)
        for s in range(u):
            ks, bs = k[r0 + s:r0 + s + 1], b[r0 + s:r0 + s + 1]
            w = jnp.sum(qi * ks * jnp.exp(jnp.minimum(bi - bs, 0.0)), axis=-1, keepdims=True)
            a_i = jnp.where((col == r0 + s) & (row_u >= s), w, a_i)
        parts.append(a_i)
    return parts[0] if len(parts) == 1 else jnp.concatenate(parts, axis=0)


def _gla_body(q_ref, k_ref, v_ref, r_ref, gl_ref, wg_ref, bg_ref, tri_ref, gn_ref, s0_ref,
              o_ref, so_ref, s_ref, *, c, u, n_chunks):
    t = pl.program_id(1)

    @pl.when(t == 0)
    def _():
        s_ref[...] = s0_ref[...]

    tri = tri_ref[...]
    scale = GLA_DK ** -0.5

    def chunk(ci, carry):
        rows = pl.ds(pl.multiple_of(ci * c, c), c)
        x = jnp.dot(gl_ref[rows, :].astype(BF16), wg_ref[...], preferred_element_type=F32) + bg_ref[...]
        log_a = (jnp.minimum(x, 0.0) - jnp.log1p(jnp.exp(-jnp.abs(x)))) * (1.0 / GLA_GATE_TAU)
        b_all = _cumsum_rows(tri, log_a)
        for hd in range(GLA_HEADS):
            kc = slice(hd * GLA_DK, (hd + 1) * GLA_DK)
            vc = slice(hd * GLA_DV, (hd + 1) * GLA_DV)
            q = q_ref[rows, kc] * scale
            k = k_ref[rows, kc]
            b = b_all[:, kc]
            v = v_ref[rows, vc].astype(BF16)
            state = s_ref[hd]
            o = jnp.dot((q * jnp.exp(b)).astype(BF16), state.astype(BF16), preferred_element_type=F32)
            a = _gla_intra(q, k, b, c, u)
            o = o + jnp.dot(a.astype(BF16), v, preferred_element_type=F32)
            b_last = b[c - 1:c]
            kt = (k * jnp.exp(b_last - b)).astype(BF16)
            decay = jnp.broadcast_to(jnp.exp(b_last), (SUBLANES, GLA_DK)).T[:, :1]
            s_ref[hd] = state * decay + lax.dot_general(kt, v, _TN, preferred_element_type=F32)
            rr = r_ref[rows, vc]
            o_ref[rows, vc] = o * _rms_scale(o) * gn_ref[:, vc] * _silu(rr)
        return carry

    lax.fori_loop(0, n_chunks, chunk, 0)

    @pl.when(t == pl.num_programs(1) - 1)
    def _():
        so_ref[...] = s_ref[...]


def gla_mix(proj, w_gate, b_gate, g_out, s0, layer, batch, seq, c, u, tt, name):
    n = batch * seq
    nt = seq // tt
    hk, hv = GLA_HEADS * GLA_DK, GLA_HEADS * GLA_DV
    tri = jnp.tril(jnp.ones((c, c), F32)).astype(BF16)
    row = lambda b, t: b * nt + t
    state_block = (None, None, GLA_HEADS, GLA_DK, GLA_DV)
    return pl.pallas_call(
        functools.partial(_gla_body, c=c, u=u, n_chunks=tt // c),
        grid=(batch, nt),
        in_specs=[pl.BlockSpec((tt, hk), lambda b, t: (row(b, t), 0)),
                  pl.BlockSpec((tt, hk), lambda b, t: (row(b, t), 1)),
                  pl.BlockSpec((tt, hv), lambda b, t: (row(b, t), 1)),
                  pl.BlockSpec((tt, hv), lambda b, t: (row(b, t), 2)),
                  pl.BlockSpec((tt, LANES), lambda b, t: (row(b, t), (2 * hk + 2 * hv) // LANES)),
                  pl.BlockSpec(w_gate.shape, lambda b, t: (0, 0)),
                  pl.BlockSpec((1, hk), lambda b, t: (0, 0)),
                  pl.BlockSpec((c, c), lambda b, t: (0, 0)),
                  pl.BlockSpec((1, hv), lambda b, t: (0, 0)),
                  pl.BlockSpec(state_block, lambda b, t: (layer, b, 0, 0, 0))],
        out_specs=[pl.BlockSpec((tt, hv), lambda b, t: (row(b, t), 0)),
                   pl.BlockSpec((None, GLA_HEADS, GLA_DK, GLA_DV), lambda b, t: (b, 0, 0, 0))],
        out_shape=[jax.ShapeDtypeStruct((n, hv), F32),
                   jax.ShapeDtypeStruct((batch, GLA_HEADS, GLA_DK, GLA_DV), F32)],
        scratch_shapes=[pltpu.VMEM((GLA_HEADS, GLA_DK, GLA_DV), F32)],
        compiler_params=_params(2),
        name=name,
    )(proj, proj, proj, proj, proj, w_gate, b_gate, tri, g_out, s0)


def kernel(x_prompt, x_sample, cache_k, cache_v, state_gla, page_table, norm_mixer, norm_ffn, w_qkv, q_norm,
           k_norm, w_attn_o, w_gla_in, w_gla_gate, b_gla_gate, gla_norm, w_gla_o, w_ffn_in, w_ffn_out):
    bp, tp, d = x_prompt.shape
    bs, ts, _ = x_sample.shape
    xp = x_prompt.reshape(bp * tp, d)
    xs = x_sample.reshape(bs * ts, d)
    n_layers_attn, n_pool = cache_k.shape[:2]
    ckt = jnp.transpose(cache_k, (0, 1, 3, 4, 2)).reshape(n_layers_attn, n_pool, d, PAGE_SIZE)
    cvt = jnp.transpose(cache_v, (0, 1, 3, 4, 2)).reshape(n_layers_attn, n_pool, d, PAGE_SIZE)
    hk, hv = GLA_HEADS * GLA_DK, GLA_HEADS * GLA_DV
    zero_state = jnp.zeros((1, bp, GLA_HEADS, GLA_DK, GLA_DV), F32)
    tm_p, tm_s = 1024, 512
    head_shape = (ATTN_HEADS, HEAD_DIM)

    k_rows_p, v_rows_p, k_rows_s, v_rows_s, st_p, st_s = [], [], [], [], [], []
    for i in range(DEPTH):
        j = i // 2
        g_mix = norm_mixer[i][None]
        if i % 2 == 0:
            w = w_qkv[j].astype(BF16)
            q_p, k_p, v_p = norm_matmul(xp, g_mix, w, tm_p, d, f"qkv_p{i}", split=True)
            q_s, k_s, v_s = norm_matmul(xs, g_mix, w, tm_s, d, f"qkv_s{i}", split=True)
            gq = jnp.tile(q_norm[j], ATTN_HEADS)[None]
            gk = jnp.tile(k_norm[j], ATTN_HEADS)[None]
            mp, kn_p = moba_prompt(q_p, k_p, v_p, gq[:, :LANES], gk[:, :LANES], bp, tp, f"moba_p{i}")
            ms, kn_s = moba_sample(q_s, k_s, v_s, gq, gk, ckt, cvt, j, page_table, f"moba_s{i}")
            k_rows_p.append(kn_p.reshape(bp, tp, *head_shape))
            v_rows_p.append(v_p.reshape(bp, tp, *head_shape))
            k_rows_s.append(kn_s.reshape(bs, ts, *head_shape))
            v_rows_s.append(v_s.reshape(bs, ts, *head_shape))
            wo = w_attn_o[j].astype(BF16)
        else:
            w_main = w_gla_in[j][:, :2 * hk + 2 * hv]
            w_low = jnp.pad(w_gla_in[j][:, 2 * hk + 2 * hv:], ((0, 0), (0, LANES - GLA_GATE_RANK)))
            w = jnp.concatenate([w_main, w_low], axis=1).astype(BF16)
            proj_p = norm_matmul(xp, g_mix, w, tm_p, 640, f"gla_in_p{i}")
            proj_s = norm_matmul(xs, g_mix, w, tm_s, 640, f"gla_in_s{i}")
            w_gate = jnp.pad(w_gla_gate[j], ((0, LANES - GLA_GATE_RANK), (0, 0))).astype(BF16)
            b_gate = b_gla_gate[j][None]
            g_out = jnp.tile(gla_norm[j], GLA_HEADS)[None]
            mp, sp = gla_mix(proj_p, w_gate, b_gate, g_out, zero_state, 0, bp, tp,
                             GLA_CHUNK, GLA_SUB, 256, f"gla_p{i}")
            ms, ss = gla_mix(proj_s, w_gate, b_gate, g_out, state_gla, j, bs, ts, ts, ts, ts, f"gla_s{i}")
            st_p.append(sp)
            st_s.append(ss)
            wo = w_gla_o[j].astype(BF16)
        g_ffn = norm_ffn[i][None]
        w_in = w_ffn_in[i].astype(BF16)
        w_out = w_ffn_out[i].astype(BF16)
        xp = post_ffn(xp, mp, wo, g_ffn, w_in, w_out, tm_p, 256, f"ffn_p{i}")
        xs = post_ffn(xs, ms, wo, g_ffn, w_in, w_out, tm_s, 256, f"ffn_s{i}")
    return (xp.reshape(bp, tp, d), xs.reshape(bs, ts, d), jnp.stack(k_rows_p), jnp.stack(v_rows_p),
            jnp.stack(k_rows_s), jnp.stack(v_rows_s), jnp.stack(st_p), jnp.stack(st_s))
```

```python
import functools

import jax
import jax.numpy as jnp
from jax import lax
from jax.experimental import pallas as pl
from jax.experimental.pallas import tpu as pltpu

F32, BF16 = jnp.float32, jnp.bfloat16

D_MODEL = 1024
DEPTH = 4
ATTN_HEADS = 16
HEAD_DIM = 64
MOBA_BLOCK = 256
MOBA_TOPK = 3
PAGE_SIZE = 128
GLA_HEADS = 4
GLA_DK = 128
GLA_DV = 256
GLA_GATE_RANK = 16
GLA_GATE_TAU = 16.0
GLA_CHUNK = 64
FFN_HIDDEN = 2816
NORM_EPS = 1e-6
NEG_INF = -1e30

LANES = 128
SUBLANES = 8
VMEM_LIMIT_BYTES = 56 * 1024 * 1024

HEADS_PER_SLAB = LANES // HEAD_DIM
N_SLABS = ATTN_HEADS // HEADS_PER_SLAB
SAMPLE_GROUP_HEADS = 4

_NT = (((1,), (1,)), ((), ()))
_TN = (((0,), (0,)), ((), ()))


def _params(n_axes):
    return pltpu.CompilerParams(dimension_semantics=("arbitrary",) * n_axes,
                                vmem_limit_bytes=VMEM_LIMIT_BYTES)


def _rms_scale(x):
    return lax.rsqrt(jnp.mean(x * x, axis=-1, keepdims=True) + NORM_EPS)


def _silu(x):
    return x / (1.0 + jnp.exp(-x))


def _norm_matmul_body(x_ref, g_ref, w_ref, *rest):
    o_refs, h_ref = rest[:-1], rest[-1]
    j = pl.program_id(1)

    @pl.when(j == 0)
    def _():
        x = x_ref[...]
        h_ref[...] = (x * _rms_scale(x) * g_ref[...]).astype(BF16)

    y = jnp.dot(h_ref[...], w_ref[...], preferred_element_type=F32)
    if len(o_refs) == 1:
        o_refs[0][...] = y
    else:
        for c, o_ref in enumerate(o_refs):
            @pl.when(j == c)
            def _(o_ref=o_ref):
                o_ref[...] = y


def norm_matmul(x, g, w, tm, tn, name, split=False):
    n, d = x.shape
    nc = w.shape[1]
    if split:
        out_specs = [pl.BlockSpec((tm, tn), lambda i, j: (i, 0))] * (nc // tn)
        out_shape = [jax.ShapeDtypeStruct((n, tn), F32)] * (nc // tn)
    else:
        out_specs = pl.BlockSpec((tm, tn), lambda i, j: (i, j))
        out_shape = jax.ShapeDtypeStruct((n, nc), F32)
    return pl.pallas_call(
        _norm_matmul_body,
        grid=(n // tm, nc // tn),
        in_specs=[pl.BlockSpec((tm, d), lambda i, j: (i, 0)),
                  pl.BlockSpec((1, d), lambda i, j: (0, 0)),
                  pl.BlockSpec((d, tn), lambda i, j: (0, j))],
        out_specs=out_specs,
        out_shape=out_shape,
        scratch_shapes=[pltpu.VMEM((tm, d), BF16)],
        compiler_params=_params(2),
        name=name,
    )(x, g, w)


def _post_ffn_body(x_ref, m_ref, wo_ref, g_ref, wg_ref, wu_ref, wd_ref, out_ref, x1_ref, h_ref, acc_ref):
    j = pl.program_id(1)

    @pl.when(j == 0)
    def _():
        x1 = x_ref[...] + jnp.dot(m_ref[...].astype(BF16), wo_ref[...], preferred_element_type=F32)
        x1_ref[...] = x1
        h_ref[...] = (x1 * _rms_scale(x1) * g_ref[...]).astype(BF16)
        acc_ref[...] = jnp.zeros_like(acc_ref)

    h = h_ref[...]
    gate = jnp.dot(h, wg_ref[...], preferred_element_type=F32)
    up = jnp.dot(h, wu_ref[...], preferred_element_type=F32)
    act = (_silu(gate) * up).astype(BF16)
    acc_ref[...] += jnp.dot(act, wd_ref[...], preferred_element_type=F32)

    @pl.when(j == pl.num_programs(1) - 1)
    def _():
        out_ref[...] = x1_ref[...] + acc_ref[...]


def post_ffn(x, m, wo, g, w_in, w_out, tm, th, name):
    n, d = x.shape
    nh = FFN_HIDDEN // th
    return pl.pallas_call(
        _post_ffn_body,
        grid=(n // tm, nh),
        in_specs=[pl.BlockSpec((tm, d), lambda i, j: (i, 0)),
                  pl.BlockSpec((tm, m.shape[1]), lambda i, j: (i, 0)),
                  pl.BlockSpec(wo.shape, lambda i, j: (0, 0)),
                  pl.BlockSpec((1, d), lambda i, j: (0, 0)),
                  pl.BlockSpec((d, th), lambda i, j: (0, j)),
                  pl.BlockSpec((d, th), lambda i, j: (0, j + nh)),
                  pl.BlockSpec((th, d), lambda i, j: (j, 0))],
        out_specs=pl.BlockSpec((tm, d), lambda i, j: (i, 0)),
        out_shape=jax.ShapeDtypeStruct((n, d), F32),
        scratch_shapes=[pltpu.VMEM((tm, d), F32), pltpu.VMEM((tm, d), BF16), pltpu.VMEM((tm, d), F32)],
        compiler_params=_params(2),
        name=name,
    )(x, m, wo, g, w_in, w_in, w_out)


def _head_pair_norm(y, gain, lo):
    y2 = y * y
    s_lo = jnp.sum(jnp.where(lo, y2, 0.0), axis=-1, keepdims=True)
    s_hi = jnp.sum(jnp.where(lo, 0.0, y2), axis=-1, keepdims=True)
    ms = jnp.where(lo, s_lo, s_hi) * (1.0 / HEAD_DIM)
    return y * lax.rsqrt(ms + NORM_EPS) * gain


def _top_blocks(gate, axis):
    nb = gate.shape[axis]
    idx = lax.broadcasted_iota(jnp.int32, gate.shape, axis)
    rank = jnp.zeros(gate.shape, jnp.int32)
    for m in range(nb):
        c = lax.slice_in_dim(gate, m, m + 1, axis=axis)
        beats = (c > gate) | ((c == gate) & (idx > m))
        rank = rank + beats.astype(jnp.int32)
    return rank < MOBA_TOPK


def _moba_prompt_body(q_ref, k_ref, v_ref, gq_ref, gk_ref, *rest, n_prev):
    if n_prev:
        pk_ref, pv_ref, o_ref, kh_ref, vh_ref, kb_ref, vt_ref, km_ref, s_ref = rest
    else:
        o_ref, kh_ref, kb_ref, vt_ref, km_ref, s_ref = rest
    i = pl.program_id(2)
    nb = k_ref.shape[0] // MOBA_BLOCK
    tq = q_ref.shape[0]
    lo = lax.broadcasted_iota(jnp.int32, (1, LANES), 1) < HEAD_DIM

    @pl.when(i == 0)
    def _():
        kn = _head_pair_norm(k_ref[...], gk_ref[...], lo)
        kh_ref[n_prev] = kn
        if n_prev:
            kh_ref[:n_prev] = pk_ref[...]
            vh_ref[:n_prev] = pv_ref[...]
            vh_ref[n_prev] = v_ref[...]
        km_ref[...] = jnp.sum(kn.reshape(nb, MOBA_BLOCK, LANES), axis=1) * (1.0 / MOBA_BLOCK)
        for n in range(nb):
            rows = slice(n * MOBA_BLOCK, (n + 1) * MOBA_BLOCK)
            kb_ref[n] = kn[rows].astype(BF16)
            vt_ref[n] = v_ref[rows, :].T.astype(BF16)

    qn = _head_pair_norm(q_ref[...], gq_ref[...], lo)
    qt = (qn * (HEAD_DIM ** -0.5)).T.astype(BF16)
    kmb = km_ref[...].astype(BF16)
    dim = lax.broadcasted_iota(jnp.int32, (LANES, 1), 0)
    zero = jnp.zeros_like(qt)
    qcat = jnp.concatenate([jnp.where(dim < HEAD_DIM, qt, zero), jnp.where(dim >= HEAD_DIM, qt, zero)], axis=1)
    blk = lax.broadcasted_iota(jnp.int32, (nb, 2 * tq), 0)
    past = blk < i
    gate = jnp.dot(kmb, qcat, preferred_element_type=F32)
    gate = jnp.where(past, gate, NEG_INF)
    sel = jnp.logical_and(_top_blocks(gate, 0), past).astype(F32)

    key = lax.broadcasted_iota(jnp.int32, (MOBA_BLOCK, 2 * tq), 0)
    qry = jnp.bitwise_and(lax.broadcasted_iota(jnp.int32, (MOBA_BLOCK, 2 * tq), 1), tq - 1)
    s_own = jnp.dot(kb_ref[i], qcat, preferred_element_type=F32)
    s_own = jnp.where(key <= qry, s_own, NEG_INF)
    m_own = jnp.max(s_own, axis=0, keepdims=True)

    def attend(n_blocks):
        m = m_own
        for n in range(n_blocks):
            rows = slice(n * MOBA_BLOCK, (n + 1) * MOBA_BLOCK)
            s = jnp.dot(kb_ref[n], qcat, preferred_element_type=F32)
            s = jnp.where(sel[n:n + 1, :] > 0.5, s, NEG_INF)
            s_ref[rows, :] = s
            m = jnp.maximum(m, jnp.max(s, axis=0, keepdims=True))
        p = jnp.exp(s_own - m)
        l = jnp.sum(p, axis=0, keepdims=True)
        acc = jnp.dot(vt_ref[i], p.astype(BF16), preferred_element_type=F32)
        for n in range(n_blocks):
            rows = slice(n * MOBA_BLOCK, (n + 1) * MOBA_BLOCK)
            p = jnp.exp(s_ref[rows, :] - m)
            l = l + jnp.sum(p, axis=0, keepdims=True)
            acc = acc + jnp.dot(vt_ref[n], p.astype(BF16), preferred_element_type=F32)
        o = acc / l
        o_ref[...] = jnp.concatenate([o[:HEAD_DIM, :tq], o[HEAD_DIM:, tq:]], axis=0).T

    variants = sorted(set(range(1, nb, 2)) | {nb - 1})
    for idx, n_blocks in enumerate(variants):
        first = variants[idx - 1] + 1 if idx else 0
        pl.when(jnp.logical_and(i >= first, i <= n_blocks))(functools.partial(attend, n_blocks))


def moba_prompt(q, k, v, gq, gk, batch, seq, name, prev_k=None, prev_v=None):
    n = batch * seq
    nb = seq // MOBA_BLOCK
    n_prev = 0 if prev_k is None else prev_k.shape[0]
    seq_rows = lambda b, h, i: (b, h)
    hist_rows = lambda b, h, i: (0, b, h)
    in_specs = [pl.BlockSpec((MOBA_BLOCK, LANES), lambda b, h, i: (b * nb + i, h)),
                pl.BlockSpec((seq, LANES), seq_rows),
                pl.BlockSpec((seq, LANES), seq_rows),
                pl.BlockSpec((1, LANES), lambda b, h, i: (0, 0)),
                pl.BlockSpec((1, LANES), lambda b, h, i: (0, 0))]
    out_specs = [pl.BlockSpec((MOBA_BLOCK, LANES), lambda b, h, i: (b * nb + i, h)),
                 pl.BlockSpec((n_prev + 1, seq, LANES), hist_rows)]
    out_shape = [jax.ShapeDtypeStruct((n, D_MODEL), F32), jax.ShapeDtypeStruct((n_prev + 1, n, D_MODEL), F32)]
    args = [q, k, v, gq, gk]
    if n_prev:
        in_specs += [pl.BlockSpec((n_prev, seq, LANES), hist_rows)] * 2
        out_specs.append(out_specs[-1])
        out_shape.append(out_shape[-1])
        args += [prev_k, prev_v]
    return pl.pallas_call(
        functools.partial(_moba_prompt_body, n_prev=n_prev),
        grid=(batch, N_SLABS, nb),
        in_specs=in_specs,
        out_specs=out_specs,
        out_shape=out_shape,
        scratch_shapes=[pltpu.VMEM((nb, MOBA_BLOCK, LANES), BF16), pltpu.VMEM((nb, LANES, MOBA_BLOCK), BF16),
                        pltpu.VMEM((nb, LANES), F32),
                        pltpu.VMEM(((nb - 1) * MOBA_BLOCK, HEADS_PER_SLAB * MOBA_BLOCK), F32)],
        compiler_params=_params(3),
        name=name,
    )(*args)


def _moba_sample_body(pt_ref, q_ref, k_ref, v_ref, gq_ref, gk_ref, hm_ref, hsel_ref, *rest, n_pages, n_prev):
    del pt_ref
    k_pages, v_pages = rest[:n_pages], rest[n_pages:2 * n_pages]
    if n_prev:
        pk_ref, pv_ref, o_ref, kh_ref, vh_ref, s_ref = rest[2 * n_pages:]
        kh_ref[:n_prev] = pk_ref[...]
        vh_ref[:n_prev] = pv_ref[...]
        vh_ref[n_prev] = v_ref[...]
    else:
        o_ref, kh_ref, s_ref = rest[2 * n_pages:]
    ppb = MOBA_BLOCK // PAGE_SIZE
    nb = n_pages // ppb
    s_new = q_ref.shape[0]
    rows = ATTN_HEADS * s_new
    lo = lax.broadcasted_iota(jnp.int32, (1, LANES), 1) < HEAD_DIM

    def heads_norm(y, gain):
        slabs = [_head_pair_norm(y[:, c * LANES:(c + 1) * LANES], gain[:, c * LANES:(c + 1) * LANES], lo)
                 for c in range(N_SLABS)]
        return jnp.concatenate(slabs, axis=1)

    qn = heads_norm(q_ref[...], gq_ref[...])
    kn = heads_norm(k_ref[...], gk_ref[...])
    kh_ref[n_prev] = kn
    qs = qn * (HEAD_DIM ** -0.5)
    gh, gw = hm_ref.shape[0] // s_new, hm_ref.shape[1]
    gr = gh * s_new
    n_groups = ATTN_HEADS // gh
    hm = hm_ref[...]
    q_groups = [(jnp.broadcast_to(qs[None, :, g * gw:(g + 1) * gw], (gh, s_new, gw)).reshape(gr, gw) * hm).astype(BF16)
                for g in range(n_groups)]

    for p in range(n_pages):
        for g in range(n_groups):
            s_ref[g * gr:(g + 1) * gr, p * PAGE_SIZE:(p + 1) * PAGE_SIZE] = jnp.dot(
                q_groups[g], k_pages[p][g * gw:(g + 1) * gw, :].astype(BF16), preferred_element_type=F32)

    def block_scores(n):
        return s_ref[:, n * MOBA_BLOCK:(n + 1) * MOBA_BLOCK]

    blk = lax.broadcasted_iota(jnp.int32, (rows, nb), 1)
    gate = jnp.zeros((rows, nb), F32)
    for n in range(nb):
        g_n = jnp.sum(block_scores(n), axis=-1, keepdims=True) * (1.0 / MOBA_BLOCK)
        gate = jnp.where(blk == n, g_n, gate)
    sel = _top_blocks(gate, 1)

    knb = kn.astype(BF16)
    s_own = jnp.concatenate(
        [lax.dot_general(q_groups[g], knb[:, g * gw:(g + 1) * gw], _NT, preferred_element_type=F32)
         for g in range(n_groups)], axis=0)
    q_pos = jnp.bitwise_and(lax.broadcasted_iota(jnp.int32, (rows, s_new), 0), s_new - 1)
    k_pos = lax.broadcasted_iota(jnp.int32, (rows, s_new), 1)
    s_own = jnp.where(k_pos <= q_pos, s_own, NEG_INF)

    def masked(n):
        return jnp.where(sel[:, n:n + 1], block_scores(n), NEG_INF)

    m = jnp.max(s_own, axis=-1, keepdims=True)
    for n in range(nb):
        m = jnp.maximum(m, jnp.max(masked(n), axis=-1, keepdims=True))
    p_own = jnp.exp(s_own - m)
    l = jnp.sum(p_own, axis=-1, keepdims=True)
    pob, vb = p_own.astype(BF16), v_ref[...].astype(BF16)
    accs = [jnp.dot(pob[g * gr:(g + 1) * gr], vb[:, g * gw:(g + 1) * gw], preferred_element_type=F32)
            for g in range(n_groups)]
    for n in range(nb):
        p = jnp.exp(masked(n) - m)
        l = l + jnp.sum(p, axis=-1, keepdims=True)
        pb = p.astype(BF16)
        for c in range(ppb):
            page = v_pages[n * ppb + c]
            for g in range(n_groups):
                accs[g] = accs[g] + lax.dot_general(
                    pb[g * gr:(g + 1) * gr, c * PAGE_SIZE:(c + 1) * PAGE_SIZE],
                    page[g * gw:(g + 1) * gw, :].astype(BF16), _NT, preferred_element_type=F32)
    hsel = hsel_ref[...]
    outs = []
    for g in range(n_groups):
        o = accs[g] / l[g * gr:(g + 1) * gr]
        outs.append(jnp.sum(o.reshape(gh, s_new, gw) * hsel[:, None, :], axis=0))
    o_ref[...] = jnp.concatenate(outs, axis=1)


def moba_sample(q, k, v, gq, gk, cache_kt, cache_vt, layer, page_table, name, prev_k=None, prev_v=None):
    dec_batch, n_pages = page_table.shape
    n_prev = 0 if prev_k is None else prev_k.shape[0]
    n = q.shape[0]
    s_new = n // dec_batch
    assert s_new & (s_new - 1) == 0, "new-token count must be a power of two"
    rows = ATTN_HEADS * s_new
    gh = SAMPLE_GROUP_HEADS
    gw = gh * HEAD_DIM
    head_of_lane = jnp.arange(gw) // HEAD_DIM
    hm = (jnp.arange(gh * s_new)[:, None] // s_new == head_of_lane[None, :]).astype(F32)
    hsel = (jnp.arange(gh)[:, None] == head_of_lane[None, :]).astype(F32)

    def page_spec(p):
        return pl.BlockSpec((None, None, D_MODEL, PAGE_SIZE),
                            lambda b, pt: (layer, pt[b * n_pages + p], 0, 0))

    const = lambda b, pt: (0, 0)
    token = pl.BlockSpec((s_new, D_MODEL), lambda b, pt: (b, 0))
    hist = lambda layers: pl.BlockSpec((layers, s_new, D_MODEL), lambda b, pt: (0, b, 0))
    in_specs = ([token, token, token,
                 pl.BlockSpec((1, D_MODEL), const),
                 pl.BlockSpec((1, D_MODEL), const),
                 pl.BlockSpec(hm.shape, const),
                 pl.BlockSpec(hsel.shape, const)]
                + [page_spec(p) for p in range(n_pages)] * 2)
    out_specs = [token, hist(n_prev + 1)]
    out_shape = [jax.ShapeDtypeStruct((n, D_MODEL), F32), jax.ShapeDtypeStruct((n_prev + 1, n, D_MODEL), F32)]
    args = [page_table.reshape(-1), q, k, v, gq, gk, hm, hsel, *([cache_kt] * n_pages), *([cache_vt] * n_pages)]
    if n_prev:
        in_specs += [hist(n_prev)] * 2
        out_specs.append(out_specs[-1])
        out_shape.append(out_shape[-1])
        args += [prev_k, prev_v]
    grid_spec = pltpu.PrefetchScalarGridSpec(
        num_scalar_prefetch=1,
        grid=(dec_batch,),
        in_specs=in_specs,
        out_specs=out_specs,
        scratch_shapes=[pltpu.VMEM((rows, n_pages * PAGE_SIZE), F32)],
    )
    return pl.pallas_call(
        functools.partial(_moba_sample_body, n_pages=n_pages, n_prev=n_prev),
        grid_spec=grid_spec,
        out_shape=out_shape,
        compiler_params=_params(1),
        name=name,
    )(*args)


def _cumsum_rows(tri, a):
    hi = a.astype(BF16)
    r1 = a - hi.astype(F32)
    mid = r1.astype(BF16)
    lo = (r1 - mid.astype(F32)).astype(BF16)
    n = a.shape[1]
    r = jnp.dot(tri, jnp.concatenate([hi, mid, lo], axis=1), preferred_element_type=F32)
    return r[:, :n] + r[:, n:2 * n] + r[:, 2 * n:]


def _boundary_rows(b, w):
    c, d = b.shape
    half = w // 2
    if half >= SUBLANES:
        return jnp.concatenate([jnp.broadcast_to(b[j * w + half - 1:j * w + half], (w, d)) for j in range(c // w)], axis=0)
    b3 = b.reshape(c // SUBLANES, SUBLANES, d)
    sub = lax.broadcasted_iota(jnp.int32, (1, SUBLANES, 1), 1)
    picks = [b3[:, j * w + half - 1:j * w + half, :] for j in range(SUBLANES // w)]
    ref = picks[-1]
    for j in range(SUBLANES // w - 2, -1, -1):
        ref = jnp.where(sub < (j + 1) * w, picks[j], ref)
    return jnp.broadcast_to(ref, b3.shape).reshape(c, d)


def _gla_intra(q, k, b, c):
    heads = [slice(h * GLA_DK, (h + 1) * GLA_DK) for h in range(q.shape[1] // GLA_DK)]
    row = lax.broadcasted_iota(jnp.int32, (c, 1), 0)
    rr = lax.broadcasted_iota(jnp.int32, (c, c), 0)
    cc = lax.broadcasted_iota(jnp.int32, (c, c), 1)
    apart = jnp.bitwise_xor(rr, cc)
    qk = q * k
    a = [jnp.where(rr == cc, jnp.sum(qk[:, h], axis=-1, keepdims=True), 0.0) for h in heads]
    w = c
    while w >= 2:
        d = b - _boundary_rows(b, w)
        f = jnp.exp(jnp.minimum(d, -d))
        lower = jnp.bitwise_and(row, w // 2) != 0
        qt = jnp.where(lower, q * f, 0.0).astype(BF16)
        kt = jnp.where(lower, 0.0, k * f).astype(BF16)
        for i, h in enumerate(heads):
            a_w = lax.dot_general(qt[:, h], kt[:, h], _NT, preferred_element_type=F32)
            a[i] = a[i] + (a_w if w == c else jnp.where(apart < w, a_w, 0.0))
        w //= 2
    return a


def _gla_body(q_ref, k_ref, v_ref, r_ref, gl_ref, wg_ref, bg_ref, tri_ref, gn_ref, s0_ref, *rest,
              c, n_chunks, n_prev):
    if n_prev:
        ps_ref, o_ref, so_ref, s_ref = rest
    else:
        o_ref, so_ref, s_ref = rest
    t = pl.program_id(1)

    @pl.when(t == 0)
    def _():
        s_ref[...] = s0_ref[...]

    tri = tri_ref[...]
    scale = GLA_DK ** -0.5

    def chunk(ci, carry):
        rows = pl.ds(pl.multiple_of(ci * c, c), c)
        x = jnp.dot(gl_ref[rows, :].astype(BF16), wg_ref[...], preferred_element_type=F32) + bg_ref[...]
        log_a = (jnp.minimum(x, 0.0) - jnp.log1p(jnp.exp(-jnp.abs(x)))) * (1.0 / GLA_GATE_TAU)
        b = _cumsum_rows(tri, log_a)
        q = q_ref[rows, :] * scale
        k = k_ref[rows, :]
        v = v_ref[rows, :].astype(BF16)
        b_last = b[c - 1:c]
        q_in = (q * jnp.exp(b)).astype(BF16)
        k_out = (k * jnp.exp(b_last - b)).astype(BF16)
        decay = jnp.broadcast_to(jnp.exp(b_last), (SUBLANES, b.shape[1])).T[:, :1]
        a = _gla_intra(q, k, b, c)
        outs = []
        for hd in range(GLA_HEADS):
            kc = slice(hd * GLA_DK, (hd + 1) * GLA_DK)
            vc = slice(hd * GLA_DV, (hd + 1) * GLA_DV)
            state = s_ref[hd]
            o = jnp.dot(q_in[:, kc], state.astype(BF16), preferred_element_type=F32)
            o = o + jnp.dot(a[hd].astype(BF16), v[:, vc], preferred_element_type=F32)
            s_ref[hd] = state * decay[kc] + lax.dot_general(k_out[:, kc], v[:, vc], _TN, preferred_element_type=F32)
            outs.append(o * _rms_scale(o))
        o_ref[rows, :] = jnp.concatenate(outs, axis=1) * gn_ref[...] * _silu(r_ref[rows, :])
        return carry

    lax.fori_loop(0, n_chunks, chunk, 0, unroll=True)

    @pl.when(t == pl.num_programs(1) - 1)
    def _():
        if n_prev:
            so_ref[:n_prev] = ps_ref[...]
        so_ref[n_prev] = s_ref[...]


def gla_mix(proj, w_gate, b_gate, g_out, s0, layer, batch, seq, c, tt, name, prev_states=None):
    n = batch * seq
    nt = seq // tt
    hk, hv = GLA_HEADS * GLA_DK, GLA_HEADS * GLA_DV
    n_prev = 0 if prev_states is None else prev_states.shape[0]
    tri = jnp.tril(jnp.ones((c, c), F32)).astype(BF16)
    row = lambda b, t: b * nt + t
    state_block = (None, None, GLA_HEADS, GLA_DK, GLA_DV)
    hist = lambda layers: pl.BlockSpec((layers, None, GLA_HEADS, GLA_DK, GLA_DV), lambda b, t: (0, b, 0, 0, 0))
    in_specs = [pl.BlockSpec((tt, hk), lambda b, t: (row(b, t), 0)),
                pl.BlockSpec((tt, hk), lambda b, t: (row(b, t), 1)),
                pl.BlockSpec((tt, hv), lambda b, t: (row(b, t), 1)),
                pl.BlockSpec((tt, hv), lambda b, t: (row(b, t), 2)),
                pl.BlockSpec((tt, LANES), lambda b, t: (row(b, t), (2 * hk + 2 * hv) // LANES)),
                pl.BlockSpec(w_gate.shape, lambda b, t: (0, 0)),
                pl.BlockSpec((1, hk), lambda b, t: (0, 0)),
                pl.BlockSpec((c, c), lambda b, t: (0, 0)),
                pl.BlockSpec((1, hv), lambda b, t: (0, 0)),
                pl.BlockSpec(state_block, lambda b, t: (layer, b, 0, 0, 0))]
    args = [proj, proj, proj, proj, proj, w_gate, b_gate, tri, g_out, s0]
    if n_prev:
        in_specs.append(hist(n_prev))
        args.append(prev_states)
    return pl.pallas_call(
        functools.partial(_gla_body, c=c, n_chunks=tt // c, n_prev=n_prev),
        grid=(batch, nt),
        in_specs=in_specs,
        out_specs=[pl.BlockSpec((tt, hv), lambda b, t: (row(b, t), 0)), hist(n_prev + 1)],
        out_shape=[jax.ShapeDtypeStruct((n, hv), F32),
                   jax.ShapeDtypeStruct((n_prev + 1, batch, GLA_HEADS, GLA_DK, GLA_DV), F32)],
        scratch_shapes=[pltpu.VMEM((GLA_HEADS, GLA_DK, GLA_DV), F32)],
        compiler_params=_params(2),
        name=name,
    )(*args)


def kernel(x_prompt, x_sample, cache_k, cache_v, state_gla, page_table, norm_mixer, norm_ffn, w_qkv, q_norm,
           k_norm, w_attn_o, w_gla_in, w_gla_gate, b_gla_gate, gla_norm, w_gla_o, w_ffn_in, w_ffn_out):
    bp, tp, d = x_prompt.shape
    bs, ts, _ = x_sample.shape
    xp = x_prompt.reshape(bp * tp, d)
    xs = x_sample.reshape(bs * ts, d)
    n_layers_attn, n_pool = cache_k.shape[:2]
    ckt = jnp.transpose(cache_k, (0, 1, 3, 4, 2)).reshape(n_layers_attn, n_pool, d, PAGE_SIZE)
    cvt = jnp.transpose(cache_v, (0, 1, 3, 4, 2)).reshape(n_layers_attn, n_pool, d, PAGE_SIZE)
    hk, hv = GLA_HEADS * GLA_DK, GLA_HEADS * GLA_DV
    zero_state = jnp.zeros((1, bp, GLA_HEADS, GLA_DK, GLA_DV), F32)
    tm_p, tm_s = 1024, 512
    head_shape = (ATTN_HEADS, HEAD_DIM)

    kh_p = vh_p = kh_s = vh_s = st_p = st_s = None
    for i in range(DEPTH):
        j = i // 2
        g_mix = norm_mixer[i][None]
        if i % 2 == 0:
            w = w_qkv[j].astype(BF16)
            q_p, k_p, v_p = norm_matmul(xp, g_mix, w, tm_p, d, f"qkv_p{i}", split=True)
            q_s, k_s, v_s = norm_matmul(xs, g_mix, w, tm_s, d, f"qkv_s{i}", split=True)
            gq = jnp.tile(q_norm[j], ATTN_HEADS)[None]
            gk = jnp.tile(k_norm[j], ATTN_HEADS)[None]
            if kh_p is None:
                mp, kh_p = moba_prompt(q_p, k_p, v_p, gq[:, :LANES], gk[:, :LANES], bp, tp, f"moba_p{i}")
                ms, kh_s = moba_sample(q_s, k_s, v_s, gq, gk, ckt, cvt, j, page_table, f"moba_s{i}")
                vh_p, vh_s = v_p[None], v_s[None]
            else:
                mp, kh_p, vh_p = moba_prompt(q_p, k_p, v_p, gq[:, :LANES], gk[:, :LANES], bp, tp, f"moba_p{i}",
                                             kh_p, vh_p)
                ms, kh_s, vh_s = moba_sample(q_s, k_s, v_s, gq, gk, ckt, cvt, j, page_table, f"moba_s{i}",
                                             kh_s, vh_s)
            wo = w_attn_o[j].astype(BF16)
        else:
            w_main = w_gla_in[j][:, :2 * hk + 2 * hv]
            w_low = jnp.pad(w_gla_in[j][:, 2 * hk + 2 * hv:], ((0, 0), (0, LANES - GLA_GATE_RANK)))
            w = jnp.concatenate([w_main, w_low], axis=1).astype(BF16)
            proj_p = norm_matmul(xp, g_mix, w, tm_p, 640, f"gla_in_p{i}")
            proj_s = norm_matmul(xs, g_mix, w, tm_s, 640, f"gla_in_s{i}")
            w_gate = jnp.pad(w_gla_gate[j], ((0, LANES - GLA_GATE_RANK), (0, 0))).astype(BF16)
            b_gate = b_gla_gate[j][None]
            g_out = jnp.tile(gla_norm[j], GLA_HEADS)[None]
            mp, st_p = gla_mix(proj_p, w_gate, b_gate, g_out, zero_state, 0, bp, tp,
                               GLA_CHUNK, 256, f"gla_p{i}", st_p)
            ms, st_s = gla_mix(proj_s, w_gate, b_gate, g_out, state_gla, j, bs, ts, ts, ts, f"gla_s{i}", st_s)
            wo = w_gla_o[j].astype(BF16)
        g_ffn = norm_ffn[i][None]
        w_in = w_ffn_in[i].astype(BF16)
        w_out = w_ffn_out[i].astype(BF16)
        xp = post_ffn(xp, mp, wo, g_ffn, w_in, w_out, tm_p, 256, f"ffn_p{i}")
        xs = post_ffn(xs, ms, wo, g_ffn, w_in, w_out, tm_s, 256, f"ffn_s{i}")
    rows_p = lambda h: h.reshape(h.shape[0], bp, tp, *head_shape)
    rows_s = lambda h: h.reshape(h.shape[0], bs, ts, *head_shape)
    return (xp.reshape(bp, tp, d), xs.reshape(bs, ts, d), rows_p(kh_p), rows_p(vh_p), rows_s(kh_s), rows_s(vh_s),
            st_p, st_s)
```

```python
import functools
import math

import jax
import jax.numpy as jnp
from jax import lax
from jax.experimental import pallas as pl
from jax.experimental.pallas import tpu as pltpu

F32, BF16 = jnp.float32, jnp.bfloat16

D_MODEL = 1024
DEPTH = 4
ATTN_HEADS = 16
HEAD_DIM = 64
MOBA_BLOCK = 256
MOBA_TOPK = 3
PAGE_SIZE = 128
GLA_HEADS = 4
GLA_DK = 128
GLA_DV = 256
GLA_GATE_RANK = 16
GLA_GATE_TAU = 16.0
GLA_CHUNK = 64
FFN_HIDDEN = 2816
NORM_EPS = 1e-6
NEG_INF = -1e30
LOG2E = math.log2(math.e)

LANES = 128
SUBLANES = 8
VMEM_LIMIT_BYTES = 56 * 1024 * 1024

HEADS_PER_SLAB = LANES // HEAD_DIM
N_SLABS = ATTN_HEADS // HEADS_PER_SLAB
SAMPLE_GROUP_HEADS = 4

_NT = (((1,), (1,)), ((), ()))
_TN = (((0,), (0,)), ((), ()))


def _params(n_axes):
    return pltpu.CompilerParams(dimension_semantics=("arbitrary",) * n_axes,
                                vmem_limit_bytes=VMEM_LIMIT_BYTES)


def _rms_scale(x):
    return lax.rsqrt(jnp.mean(x * x, axis=-1, keepdims=True) + NORM_EPS)


def _silu(x):
    return x / (1.0 + jnp.exp(-x))


def _norm_matmul_body(x_ref, g_ref, w_ref, *rest):
    o_refs, h_ref = rest[:-1], rest[-1]
    j = pl.program_id(1)

    @pl.when(j == 0)
    def _():
        x = x_ref[...]
        h_ref[...] = (x * _rms_scale(x) * g_ref[...]).astype(BF16)

    y = jnp.dot(h_ref[...], w_ref[...], preferred_element_type=F32)
    if len(o_refs) == 1:
        o_refs[0][...] = y
    else:
        for c, o_ref in enumerate(o_refs):
            @pl.when(j == c)
            def _(o_ref=o_ref):
                o_ref[...] = y


def norm_matmul(x, g, w, tm, tn, name, split=False):
    n, d = x.shape
    nc = w.shape[1]
    if split:
        out_specs = [pl.BlockSpec((tm, tn), lambda i, j: (i, 0))] * (nc // tn)
        out_shape = [jax.ShapeDtypeStruct((n, tn), F32)] * (nc // tn)
    else:
        out_specs = pl.BlockSpec((tm, tn), lambda i, j: (i, j))
        out_shape = jax.ShapeDtypeStruct((n, nc), F32)
    return pl.pallas_call(
        _norm_matmul_body,
        grid=(n // tm, nc // tn),
        in_specs=[pl.BlockSpec((tm, d), lambda i, j: (i, 0)),
                  pl.BlockSpec((1, d), lambda i, j: (0, 0)),
                  pl.BlockSpec((d, tn), lambda i, j: (0, j))],
        out_specs=out_specs,
        out_shape=out_shape,
        scratch_shapes=[pltpu.VMEM((tm, d), BF16)],
        compiler_params=_params(2),
        name=name,
    )(x, g, w)


def _post_ffn_body(x_ref, m_ref, wo_ref, g_ref, wg_ref, wu_ref, wd_ref, out_ref, x1_ref, h_ref, acc_ref):
    j = pl.program_id(1)

    @pl.when(j == 0)
    def _():
        x1 = x_ref[...] + jnp.dot(m_ref[...].astype(BF16), wo_ref[...], preferred_element_type=F32)
        x1_ref[...] = x1
        h_ref[...] = (x1 * _rms_scale(x1) * g_ref[...]).astype(BF16)
        acc_ref[...] = jnp.zeros_like(acc_ref)

    h = h_ref[...]
    gate = jnp.dot(h, wg_ref[...], preferred_element_type=F32)
    up = jnp.dot(h, wu_ref[...], preferred_element_type=F32)
    act = (_silu(gate) * up).astype(BF16)
    acc_ref[...] += jnp.dot(act, wd_ref[...], preferred_element_type=F32)

    @pl.when(j == pl.num_programs(1) - 1)
    def _():
        out_ref[...] = x1_ref[...] + acc_ref[...]


def post_ffn(x, m, wo, g, w_in, w_out, tm, th, name):
    n, d = x.shape
    nh = FFN_HIDDEN // th
    return pl.pallas_call(
        _post_ffn_body,
        grid=(n // tm, nh),
        in_specs=[pl.BlockSpec((tm, d), lambda i, j: (i, 0)),
                  pl.BlockSpec((tm, m.shape[1]), lambda i, j: (i, 0)),
                  pl.BlockSpec(wo.shape, lambda i, j: (0, 0)),
                  pl.BlockSpec((1, d), lambda i, j: (0, 0)),
                  pl.BlockSpec((d, th), lambda i, j: (0, j)),
                  pl.BlockSpec((d, th), lambda i, j: (0, j + nh)),
                  pl.BlockSpec((th, d), lambda i, j: (j, 0))],
        out_specs=pl.BlockSpec((tm, d), lambda i, j: (i, 0)),
        out_shape=jax.ShapeDtypeStruct((n, d), F32),
        scratch_shapes=[pltpu.VMEM((tm, d), F32), pltpu.VMEM((tm, d), BF16), pltpu.VMEM((tm, d), F32)],
        compiler_params=_params(2),
        name=name,
    )(x, m, wo, g, w_in, w_in, w_out)


def _head_pair_norm(y, gain, lo):
    y2 = y * y
    s_lo = jnp.sum(jnp.where(lo, y2, 0.0), axis=-1, keepdims=True)
    s_hi = jnp.sum(jnp.where(lo, 0.0, y2), axis=-1, keepdims=True)
    ms = jnp.where(lo, s_lo, s_hi) * (1.0 / HEAD_DIM)
    return y * lax.rsqrt(ms + NORM_EPS) * gain


def _top_blocks(gate, axis):
    nb = gate.shape[axis]
    idx = lax.broadcasted_iota(jnp.int32, gate.shape, axis)
    rank = jnp.zeros(gate.shape, jnp.int32)
    for m in range(nb):
        c = lax.slice_in_dim(gate, m, m + 1, axis=axis)
        beats = (c > gate) | ((c == gate) & (idx > m))
        rank = rank + beats.astype(jnp.int32)
    return rank < MOBA_TOPK


def _moba_prompt_body(q_ref, k_ref, v_ref, gq_ref, gk_ref, *rest, n_prev):
    if n_prev:
        pk_ref, pv_ref, o_ref, kh_ref, vh_ref, kb_ref, vt_ref, qc_ref, km_ref, s_ref = rest
    else:
        o_ref, kh_ref, vh_ref, kb_ref, vt_ref, qc_ref, km_ref, s_ref = rest
    i = pl.program_id(2)
    nb = k_ref.shape[0] // MOBA_BLOCK
    tq = MOBA_BLOCK
    lo = lax.broadcasted_iota(jnp.int32, (1, LANES), 1) < HEAD_DIM

    @pl.when(i == 0)
    def _():
        kn = _head_pair_norm(k_ref[...], gk_ref[...], lo)
        qn = _head_pair_norm(q_ref[...], gq_ref[...], lo)
        kt = kn.T
        vt = v_ref[...].T
        qt = (qn * (HEAD_DIM ** -0.5 * LOG2E)).T.astype(BF16)
        kh_ref[n_prev] = kt
        vh_ref[n_prev] = vt
        if n_prev:
            kh_ref[:n_prev] = pk_ref[...]
            vh_ref[:n_prev] = pv_ref[...]
        km_ref[...] = jnp.sum(kn.reshape(nb, MOBA_BLOCK, LANES), axis=1) * (1.0 / MOBA_BLOCK)
        dim = lax.broadcasted_iota(jnp.int32, (LANES, 1), 0)
        zero = jnp.zeros_like(qt)
        q_lo, q_hi = jnp.where(dim < HEAD_DIM, qt, zero), jnp.where(dim >= HEAD_DIM, qt, zero)
        for n in range(nb):
            span = slice(n * MOBA_BLOCK, (n + 1) * MOBA_BLOCK)
            kb_ref[n] = kn[span].astype(BF16)
            vt_ref[n] = vt[:, span].astype(BF16)
            qc_ref[n] = jnp.concatenate([q_lo[:, span], q_hi[:, span]], axis=1)

    qcat = qc_ref[i]
    kmb = km_ref[...].astype(BF16)
    blk = lax.broadcasted_iota(jnp.int32, (nb, 2 * tq), 0)
    past = blk < i
    gate = jnp.dot(kmb, qcat, preferred_element_type=F32)
    gate = jnp.where(past, gate, NEG_INF)
    sel = jnp.logical_and(_top_blocks(gate, 0), past).astype(F32)

    key = lax.broadcasted_iota(jnp.int32, (MOBA_BLOCK, 2 * tq), 0)
    qry = jnp.bitwise_and(lax.broadcasted_iota(jnp.int32, (MOBA_BLOCK, 2 * tq), 1), tq - 1)
    s_own = jnp.dot(kb_ref[i], qcat, preferred_element_type=F32)
    s_own = jnp.where(key <= qry, s_own, NEG_INF)
    m_own = jnp.max(s_own, axis=0, keepdims=True)

    def attend(n_blocks):
        m = m_own
        for n in range(n_blocks):
            rows = slice(n * MOBA_BLOCK, (n + 1) * MOBA_BLOCK)
            s = jnp.dot(kb_ref[n], qcat, preferred_element_type=F32)
            s = jnp.where(sel[n:n + 1, :] > 0.5, s, NEG_INF)
            s_ref[rows, :] = s
            m = jnp.maximum(m, jnp.max(s, axis=0, keepdims=True))
        p = jnp.exp2(s_own - m)
        l = jnp.sum(p, axis=0, keepdims=True)
        acc = jnp.dot(vt_ref[i], p.astype(BF16), preferred_element_type=F32)
        for n in range(n_blocks):
            rows = slice(n * MOBA_BLOCK, (n + 1) * MOBA_BLOCK)
            p = jnp.exp2(s_ref[rows, :] - m)
            l = l + jnp.sum(p, axis=0, keepdims=True)
            acc = acc + jnp.dot(vt_ref[n], p.astype(BF16), preferred_element_type=F32)
        o = acc / l
        o_ref[...] = jnp.concatenate([o[:HEAD_DIM, :tq], o[HEAD_DIM:, tq:]], axis=0).T

    for n_blocks in range(nb):
        pl.when(i == n_blocks)(functools.partial(attend, n_blocks))


def moba_prompt(q, k, v, gq, gk, batch, seq, name, prev_k=None, prev_v=None):
    n = batch * seq
    nb = seq // MOBA_BLOCK
    n_prev = 0 if prev_k is None else prev_k.shape[0]
    seq_rows = lambda b, h, i: (b, h)
    hist = lambda layers: pl.BlockSpec((layers, None, LANES, seq), lambda b, h, i: (0, b, h, 0))
    in_specs = [pl.BlockSpec((seq, LANES), seq_rows),
                pl.BlockSpec((seq, LANES), seq_rows),
                pl.BlockSpec((seq, LANES), seq_rows),
                pl.BlockSpec((1, LANES), lambda b, h, i: (0, 0)),
                pl.BlockSpec((1, LANES), lambda b, h, i: (0, 0))]
    hist_shape = jax.ShapeDtypeStruct((n_prev + 1, batch, D_MODEL, seq), F32)
    out_specs = [pl.BlockSpec((MOBA_BLOCK, LANES), lambda b, h, i: (b * nb + i, h)), hist(n_prev + 1), hist(n_prev + 1)]
    out_shape = [jax.ShapeDtypeStruct((n, D_MODEL), F32), hist_shape, hist_shape]
    args = [q, k, v, gq, gk]
    if n_prev:
        in_specs += [hist(n_prev)] * 2
        args += [prev_k, prev_v]
    return pl.pallas_call(
        functools.partial(_moba_prompt_body, n_prev=n_prev),
        grid=(batch, N_SLABS, nb),
        in_specs=in_specs,
        out_specs=out_specs,
        out_shape=out_shape,
        scratch_shapes=[pltpu.VMEM((nb, MOBA_BLOCK, LANES), BF16),
                        pltpu.VMEM((nb, LANES, MOBA_BLOCK), BF16),
                        pltpu.VMEM((nb, LANES, HEADS_PER_SLAB * MOBA_BLOCK), BF16),
                        pltpu.VMEM((nb, LANES), F32),
                        pltpu.VMEM(((nb - 1) * MOBA_BLOCK, HEADS_PER_SLAB * MOBA_BLOCK), F32)],
        compiler_params=_params(3),
        name=name,
    )(*args)


def _moba_sample_body(pt_ref, q_ref, k_ref, v_ref, gq_ref, gk_ref, hm_ref, hsel_ref, *rest, n_pages, n_prev):
    del pt_ref
    k_pages, v_pages = rest[:n_pages], rest[n_pages:2 * n_pages]
    if n_prev:
        pk_ref, pv_ref, o_ref, kh_ref, vh_ref, s_ref = rest[2 * n_pages:]
        kh_ref[:n_prev] = pk_ref[...]
        vh_ref[:n_prev] = pv_ref[...]
        vh_ref[n_prev] = v_ref[...]
    else:
        o_ref, kh_ref, s_ref = rest[2 * n_pages:]
    ppb = MOBA_BLOCK // PAGE_SIZE
    nb = n_pages // ppb
    s_new = q_ref.shape[0]
    rows = ATTN_HEADS * s_new
    lo = lax.broadcasted_iota(jnp.int32, (1, LANES), 1) < HEAD_DIM

    def heads_norm(y, gain):
        slabs = [_head_pair_norm(y[:, c * LANES:(c + 1) * LANES], gain[:, c * LANES:(c + 1) * LANES], lo)
                 for c in range(N_SLABS)]
        return jnp.concatenate(slabs, axis=1)

    qn = heads_norm(q_ref[...], gq_ref[...])
    kn = heads_norm(k_ref[...], gk_ref[...])
    kh_ref[n_prev] = kn
    qs = qn * (HEAD_DIM ** -0.5)
    gh, gw = hm_ref.shape[0] // s_new, hm_ref.shape[1]
    gr = gh * s_new
    n_groups = ATTN_HEADS // gh
    hm = hm_ref[...]
    q_groups = [(jnp.broadcast_to(qs[None, :, g * gw:(g + 1) * gw], (gh, s_new, gw)).reshape(gr, gw) * hm).astype(BF16)
                for g in range(n_groups)]

    for p in range(n_pages):
        for g in range(n_groups):
            s_ref[g * gr:(g + 1) * gr, p * PAGE_SIZE:(p + 1) * PAGE_SIZE] = jnp.dot(
                q_groups[g], k_pages[p][g * gw:(g + 1) * gw, :].astype(BF16), preferred_element_type=F32)

    def block_scores(n):
        return s_ref[:, n * MOBA_BLOCK:(n + 1) * MOBA_BLOCK]

    blk = lax.broadcasted_iota(jnp.int32, (rows, nb), 1)
    gate = jnp.zeros((rows, nb), F32)
    for n in range(nb):
        g_n = jnp.sum(block_scores(n), axis=-1, keepdims=True) * (1.0 / MOBA_BLOCK)
        gate = jnp.where(blk == n, g_n, gate)
    sel = _top_blocks(gate, 1)

    knb = kn.astype(BF16)
    s_own = jnp.concatenate(
        [lax.dot_general(q_groups[g], knb[:, g * gw:(g + 1) * gw], _NT, preferred_element_type=F32)
         for g in range(n_groups)], axis=0)
    q_pos = jnp.bitwise_and(lax.broadcasted_iota(jnp.int32, (rows, s_new), 0), s_new - 1)
    k_pos = lax.broadcasted_iota(jnp.int32, (rows, s_new), 1)
    s_own = jnp.where(k_pos <= q_pos, s_own, NEG_INF)

    def masked(n):
        return jnp.where(sel[:, n:n + 1], block_scores(n), NEG_INF)

    m = jnp.max(s_own, axis=-1, keepdims=True)
    for n in range(nb):
        m = jnp.maximum(m, jnp.max(masked(n), axis=-1, keepdims=True))
    p_own = jnp.exp(s_own - m)
    l = jnp.sum(p_own, axis=-1, keepdims=True)
    pob, vb = p_own.astype(BF16), v_ref[...].astype(BF16)
    accs = [jnp.dot(pob[g * gr:(g + 1) * gr], vb[:, g * gw:(g + 1) * gw], preferred_element_type=F32)
            for g in range(n_groups)]
    for n in range(nb):
        p = jnp.exp(masked(n) - m)
        l = l + jnp.sum(p, axis=-1, keepdims=True)
        pb = p.astype(BF16)
        for c in range(ppb):
            page = v_pages[n * ppb + c]
            for g in range(n_groups):
                accs[g] = accs[g] + lax.dot_general(
                    pb[g * gr:(g + 1) * gr, c * PAGE_SIZE:(c + 1) * PAGE_SIZE],
                    page[g * gw:(g + 1) * gw, :].astype(BF16), _NT, preferred_element_type=F32)
    hsel = hsel_ref[...]
    outs = []
    for g in range(n_groups):
        o = accs[g] / l[g * gr:(g + 1) * gr]
        outs.append(jnp.sum(o.reshape(gh, s_new, gw) * hsel[:, None, :], axis=0))
    o_ref[...] = jnp.concatenate(outs, axis=1)


def moba_sample(q, k, v, gq, gk, cache_kt, cache_vt, layer, page_table, name, prev_k=None, prev_v=None):
    dec_batch, n_pages = page_table.shape
    n_prev = 0 if prev_k is None else prev_k.shape[0]
    n = q.shape[0]
    s_new = n // dec_batch
    assert s_new & (s_new - 1) == 0, "new-token count must be a power of two"
    rows = ATTN_HEADS * s_new
    gh = SAMPLE_GROUP_HEADS
    gw = gh * HEAD_DIM
    head_of_lane = jnp.arange(gw) // HEAD_DIM
    hm = (jnp.arange(gh * s_new)[:, None] // s_new == head_of_lane[None, :]).astype(F32)
    hsel = (jnp.arange(gh)[:, None] == head_of_lane[None, :]).astype(F32)

    def page_spec(p):
        return pl.BlockSpec((None, None, D_MODEL, PAGE_SIZE),
                            lambda b, pt: (layer, pt[b * n_pages + p], 0, 0))

    const = lambda b, pt: (0, 0)
    token = pl.BlockSpec((s_new, D_MODEL), lambda b, pt: (b, 0))
    hist = lambda layers: pl.BlockSpec((layers, s_new, D_MODEL), lambda b, pt: (0, b, 0))
    in_specs = ([token, token, token,
                 pl.BlockSpec((1, D_MODEL), const),
                 pl.BlockSpec((1, D_MODEL), const),
                 pl.BlockSpec(hm.shape, const),
                 pl.BlockSpec(hsel.shape, const)]
                + [page_spec(p) for p in range(n_pages)] * 2)
    out_specs = [token, hist(n_prev + 1)]
    out_shape = [jax.ShapeDtypeStruct((n, D_MODEL), F32), jax.ShapeDtypeStruct((n_prev + 1, n, D_MODEL), F32)]
    args = [page_table.reshape(-1), q, k, v, gq, gk, hm, hsel, *([cache_kt] * n_pages), *([cache_vt] * n_pages)]
    if n_prev:
        in_specs += [hist(n_prev)] * 2
        out_specs.append(out_specs[-1])
        out_shape.append(out_shape[-1])
        args += [prev_k, prev_v]
    grid_spec = pltpu.PrefetchScalarGridSpec(
        num_scalar_prefetch=1,
        grid=(dec_batch,),
        in_specs=in_specs,
        out_specs=out_specs,
        scratch_shapes=[pltpu.VMEM((rows, n_pages * PAGE_SIZE), F32)],
    )
    return pl.pallas_call(
        functools.partial(_moba_sample_body, n_pages=n_pages, n_prev=n_prev),
        grid_spec=grid_spec,
        out_shape=out_shape,
        compiler_params=_params(1),
        name=name,
    )(*args)


def _cumsum_rows(tri, a):
    hi = a.astype(BF16)
    r1 = a - hi.astype(F32)
    mid = r1.astype(BF16)
    lo = (r1 - mid.astype(F32)).astype(BF16)
    n = a.shape[1]
    r = jnp.dot(tri, jnp.concatenate([hi, mid, lo], axis=1), preferred_element_type=F32)
    return r[:, :n] + r[:, n:2 * n] + r[:, 2 * n:]


def _boundary_rows(b, w):
    c, d = b.shape
    half = w // 2
    if half >= SUBLANES:
        return jnp.concatenate([jnp.broadcast_to(b[j * w + half - 1:j * w + half], (w, d)) for j in range(c // w)], axis=0)
    b3 = b.reshape(c // SUBLANES, SUBLANES, d)
    sub = lax.broadcasted_iota(jnp.int32, (1, SUBLANES, 1), 1)
    picks = [b3[:, j * w + half - 1:j * w + half, :] for j in range(SUBLANES // w)]
    ref = picks[-1]
    for j in range(SUBLANES // w - 2, -1, -1):
        ref = jnp.where(sub < (j + 1) * w, picks[j], ref)
    return jnp.broadcast_to(ref, b3.shape).reshape(c, d)


def _gla_intra(q, k, b, c):
    heads = [slice(h * GLA_DK, (h + 1) * GLA_DK) for h in range(q.shape[1] // GLA_DK)]
    row = lax.broadcasted_iota(jnp.int32, (c, 1), 0)
    rr = lax.broadcasted_iota(jnp.int32, (c, c), 0)
    cc = lax.broadcasted_iota(jnp.int32, (c, c), 1)
    apart = jnp.bitwise_xor(rr, cc)
    qk = q * k
    a = [jnp.where(rr == cc, jnp.sum(qk[:, h], axis=-1, keepdims=True), 0.0) for h in heads]
    w = c
    while w >= 2:
        d = b - _boundary_rows(b, w)
        f = jnp.exp(jnp.minimum(d, -d))
        lower = jnp.bitwise_and(row, w // 2) != 0
        qt = jnp.where(lower, q * f, 0.0).astype(BF16)
        kt = jnp.where(lower, 0.0, k * f).astype(BF16)
        for i, h in enumerate(heads):
            a_w = lax.dot_general(qt[:, h], kt[:, h], _NT, preferred_element_type=F32)
            a[i] = a[i] + (a_w if w == c else jnp.where(apart < w, a_w, 0.0))
        w //= 2
    return a


def _gla_body(q_ref, k_ref, v_ref, r_ref, gl_ref, wg_ref, bg_ref, tri_ref, gn_ref, s0_ref, *rest,
              c, n_chunks, n_prev):
    if n_prev:
        ps_ref, o_ref, so_ref, s_ref = rest
    else:
        o_ref, so_ref, s_ref = rest
    t = pl.program_id(1)

    @pl.when(t == 0)
    def _():
        s_ref[...] = s0_ref[...]

    tri = tri_ref[...]
    scale = GLA_DK ** -0.5

    def chunk(ci, carry):
        rows = pl.ds(pl.multiple_of(ci * c, c), c)
        x = jnp.dot(gl_ref[rows, :].astype(BF16), wg_ref[...], preferred_element_type=F32) + bg_ref[...]
        log_a = (jnp.minimum(x, 0.0) - jnp.log1p(jnp.exp(-jnp.abs(x)))) * (1.0 / GLA_GATE_TAU)
        b = _cumsum_rows(tri, log_a)
        q = q_ref[rows, :] * scale
        k = k_ref[rows, :]
        v = v_ref[rows, :].astype(BF16)
        b_last = b[c - 1:c]
        q_in = (q * jnp.exp(b)).astype(BF16)
        k_out = (k * jnp.exp(b_last - b)).astype(BF16)
        decay = jnp.broadcast_to(jnp.exp(b_last), (SUBLANES, b.shape[1])).T[:, :1]
        a = _gla_intra(q, k, b, c)
        outs = []
        for hd in range(GLA_HEADS):
            kc = slice(hd * GLA_DK, (hd + 1) * GLA_DK)
            vc = slice(hd * GLA_DV, (hd + 1) * GLA_DV)
            state = s_ref[hd]
            o = jnp.dot(q_in[:, kc], state.astype(BF16), preferred_element_type=F32)
            o = o + jnp.dot(a[hd].astype(BF16), v[:, vc], preferred_element_type=F32)
            s_ref[hd] = state * decay[kc] + lax.dot_general(k_out[:, kc], v[:, vc], _TN, preferred_element_type=F32)
            outs.append(o * _rms_scale(o))
        o_ref[rows, :] = jnp.concatenate(outs, axis=1) * gn_ref[...] * _silu(r_ref[rows, :])
        return carry

    lax.fori_loop(0, n_chunks, chunk, 0, unroll=True)

    @pl.when(t == pl.num_programs(1) - 1)
    def _():
        if n_prev:
            so_ref[:n_prev] = ps_ref[...]
        so_ref[n_prev] = s_ref[...]


def gla_mix(proj, w_gate, b_gate, g_out, s0, layer, batch, seq, c, tt, name, prev_states=None):
    n = batch * seq
    nt = seq // tt
    hk, hv = GLA_HEADS * GLA_DK, GLA_HEADS * GLA_DV
    n_prev = 0 if prev_states is None else prev_states.shape[0]
    tri = jnp.tril(jnp.ones((c, c), F32)).astype(BF16)
    row = lambda b, t: b * nt + t
    state_block = (None, None, GLA_HEADS, GLA_DK, GLA_DV)
    hist = lambda layers: pl.BlockSpec((layers, None, GLA_HEADS, GLA_DK, GLA_DV), lambda b, t: (0, b, 0, 0, 0))
    in_specs = [pl.BlockSpec((tt, hk), lambda b, t: (row(b, t), 0)),
                pl.BlockSpec((tt, hk), lambda b, t: (row(b, t), 1)),
                pl.BlockSpec((tt, hv), lambda b, t: (row(b, t), 1)),
                pl.BlockSpec((tt, hv), lambda b, t: (row(b, t), 2)),
                pl.BlockSpec((tt, LANES), lambda b, t: (row(b, t), (2 * hk + 2 * hv) // LANES)),
                pl.BlockSpec(w_gate.shape, lambda b, t: (0, 0)),
                pl.BlockSpec((1, hk), lambda b, t: (0, 0)),
                pl.BlockSpec((c, c), lambda b, t: (0, 0)),
                pl.BlockSpec((1, hv), lambda b, t: (0, 0)),
                pl.BlockSpec(state_block, lambda b, t: (layer, b, 0, 0, 0))]
    args = [proj, proj, proj, proj, proj, w_gate, b_gate, tri, g_out, s0]
    if n_prev:
        in_specs.append(hist(n_prev))
        args.append(prev_states)
    return pl.pallas_call(
        functools.partial(_gla_body, c=c, n_chunks=tt // c, n_prev=n_prev),
        grid=(batch, nt),
        in_specs=in_specs,
        out_specs=[pl.BlockSpec((tt, hv), lambda b, t: (row(b, t), 0)), hist(n_prev + 1)],
        out_shape=[jax.ShapeDtypeStruct((n, hv), F32),
                   jax.ShapeDtypeStruct((n_prev + 1, batch, GLA_HEADS, GLA_DK, GLA_DV), F32)],
        scratch_shapes=[pltpu.VMEM((GLA_HEADS, GLA_DK, GLA_DV), F32)],
        compiler_params=_params(2),
        name=name,
    )(*args)


def kernel(x_prompt, x_sample, cache_k, cache_v, state_gla, page_table, norm_mixer, norm_ffn, w_qkv, q_norm,
           k_norm, w_attn_o, w_gla_in, w_gla_gate, b_gla_gate, gla_norm, w_gla_o, w_ffn_in, w_ffn_out):
    bp, tp, d = x_prompt.shape
    bs, ts, _ = x_sample.shape
    xp = x_prompt.reshape(bp * tp, d)
    xs = x_sample.reshape(bs * ts, d)
    n_layers_attn, n_pool = cache_k.shape[:2]
    ckt = jnp.transpose(cache_k, (0, 1, 3, 4, 2)).reshape(n_layers_attn, n_pool, d, PAGE_SIZE)
    cvt = jnp.transpose(cache_v, (0, 1, 3, 4, 2)).reshape(n_layers_attn, n_pool, d, PAGE_SIZE)
    hk, hv = GLA_HEADS * GLA_DK, GLA_HEADS * GLA_DV
    zero_state = jnp.zeros((1, bp, GLA_HEADS, GLA_DK, GLA_DV), F32)
    tm_p, tm_s = 1024, 512
    head_shape = (ATTN_HEADS, HEAD_DIM)

    kh_p = vh_p = kh_s = vh_s = st_p = st_s = None
    for i in range(DEPTH):
        j = i // 2
        g_mix = norm_mixer[i][None]
        if i % 2 == 0:
            w = w_qkv[j].astype(BF16)
            q_p, k_p, v_p = norm_matmul(xp, g_mix, w, tm_p, d, f"qkv_p{i}", split=True)
            q_s, k_s, v_s = norm_matmul(xs, g_mix, w, tm_s, d, f"qkv_s{i}", split=True)
            gq = jnp.tile(q_norm[j], ATTN_HEADS)[None]
            gk = jnp.tile(k_norm[j], ATTN_HEADS)[None]
            mp, kh_p, vh_p = moba_prompt(q_p, k_p, v_p, gq[:, :LANES], gk[:, :LANES], bp, tp, f"moba_p{i}",
                                         kh_p, vh_p)
            if kh_s is None:
                ms, kh_s = moba_sample(q_s, k_s, v_s, gq, gk, ckt, cvt, j, page_table, f"moba_s{i}")
                vh_s = v_s[None]
            else:
                ms, kh_s, vh_s = moba_sample(q_s, k_s, v_s, gq, gk, ckt, cvt, j, page_table, f"moba_s{i}",
                                             kh_s, vh_s)
            wo = w_attn_o[j].astype(BF16)
        else:
            w_main = w_gla_in[j][:, :2 * hk + 2 * hv]
            w_low = jnp.pad(w_gla_in[j][:, 2 * hk + 2 * hv:], ((0, 0), (0, LANES - GLA_GATE_RANK)))
            w = jnp.concatenate([w_main, w_low], axis=1).astype(BF16)
            proj_p = norm_matmul(xp, g_mix, w, tm_p, 640, f"gla_in_p{i}")
            proj_s = norm_matmul(xs, g_mix, w, tm_s, 640, f"gla_in_s{i}")
            w_gate = jnp.pad(w_gla_gate[j], ((0, LANES - GLA_GATE_RANK), (0, 0))).astype(BF16)
            b_gate = b_gla_gate[j][None]
            g_out = jnp.tile(gla_norm[j], GLA_HEADS)[None]
            mp, st_p = gla_mix(proj_p, w_gate, b_gate, g_out, zero_state, 0, bp, tp,
                               GLA_CHUNK, 256, f"gla_p{i}", st_p)
            ms, st_s = gla_mix(proj_s, w_gate, b_gate, g_out, state_gla, j, bs, ts, ts, ts, f"gla_s{i}", st_s)
            wo = w_gla_o[j].astype(BF16)
        g_ffn = norm_ffn[i][None]
        w_in = w_ffn_in[i].astype(BF16)
        w_out = w_ffn_out[i].astype(BF16)
        xp = post_ffn(xp, mp, wo, g_ffn, w_in, w_out, tm_p, 256, f"ffn_p{i}")
        xs = post_ffn(xs, ms, wo, g_ffn, w_in, w_out, tm_s, 256, f"ffn_s{i}")
    rows_p = lambda h: jnp.transpose(h.reshape(h.shape[0], bp, *head_shape, tp), (0, 1, 4, 2, 3))
    rows_s = lambda h: h.reshape(h.shape[0], bs, ts, *head_shape)
    return (xp.reshape(bp, tp, d), xs.reshape(bs, ts, d), rows_p(kh_p), rows_p(vh_p), rows_s(kh_s), rows_s(vh_s),
            st_p, st_s)
```

```python
import functools
import math

import jax
import jax.numpy as jnp
from jax import lax
from jax.experimental import pallas as pl
from jax.experimental.pallas import tpu as pltpu

F32, BF16 = jnp.float32, jnp.bfloat16

D_MODEL = 1024
DEPTH = 4
ATTN_HEADS = 16
HEAD_DIM = 64
MOBA_BLOCK = 256
MOBA_TOPK = 3
PAGE_SIZE = 128
GLA_HEADS = 4
GLA_DK = 128
GLA_DV = 256
GLA_GATE_RANK = 16
GLA_GATE_TAU = 16.0
GLA_CHUNK = 64
FFN_HIDDEN = 2816
NORM_EPS = 1e-6
NEG_INF = -1e30
LOG2E = math.log2(math.e)

LANES = 128
SUBLANES = 8
VMEM_LIMIT_BYTES = 56 * 1024 * 1024

HEADS_PER_SLAB = LANES // HEAD_DIM
N_SLABS = ATTN_HEADS // HEADS_PER_SLAB
GLA_SAMPLE_SEQS = 4
SAMPLE_GROUP_HEADS = 4

_NT = (((1,), (1,)), ((), ()))
_TN = (((0,), (0,)), ((), ()))


def _params(n_axes):
    return pltpu.CompilerParams(dimension_semantics=("arbitrary",) * n_axes,
                                vmem_limit_bytes=VMEM_LIMIT_BYTES)


def _rms_scale(x):
    return lax.rsqrt(jnp.mean(x * x, axis=-1, keepdims=True) + NORM_EPS)


def _silu(x):
    return x / (1.0 + jnp.exp(-x))


def _norm_matmul_body(x_ref, g_ref, w_ref, *o_refs):
    x = x_ref[...]
    h = (x * _rms_scale(x) * g_ref[...]).astype(BF16)
    col = 0
    for o_ref in o_refs:
        width = o_ref.shape[1]
        o_ref[...] = jnp.dot(h, w_ref[:, col:col + width], preferred_element_type=F32)
        col += width


def norm_matmul(x, g, w, tm, widths, name):
    n, d = x.shape
    assert sum(widths) == w.shape[1]
    return pl.pallas_call(
        _norm_matmul_body,
        grid=(n // tm,),
        in_specs=[pl.BlockSpec((tm, d), lambda i: (i, 0)),
                  pl.BlockSpec((1, d), lambda i: (0, 0)),
                  pl.BlockSpec(w.shape, lambda i: (0, 0))],
        out_specs=[pl.BlockSpec((tm, width), lambda i: (i, 0)) for width in widths],
        out_shape=[jax.ShapeDtypeStruct((n, width), F32) for width in widths],
        compiler_params=_params(1),
        name=name,
    )(x, g, w)


def _post_ffn_body(x_ref, m_ref, wo_ref, g_ref, wg_ref, wu_ref, wd_ref, out_ref, x1_ref, h_ref, acc_ref):
    j = pl.program_id(1)

    @pl.when(j == 0)
    def _():
        x1 = x_ref[...] + jnp.dot(m_ref[...].astype(BF16), wo_ref[...], preferred_element_type=F32)
        x1_ref[...] = x1
        h_ref[...] = (x1 * _rms_scale(x1) * g_ref[...]).astype(BF16)
        acc_ref[...] = jnp.zeros_like(acc_ref)

    h = h_ref[...]
    gate = jnp.dot(h, wg_ref[...], preferred_element_type=F32)
    up = jnp.dot(h, wu_ref[...], preferred_element_type=F32)
    act = (_silu(gate) * up).astype(BF16)
    acc_ref[...] += jnp.dot(act, wd_ref[...], preferred_element_type=F32)

    @pl.when(j == pl.num_programs(1) - 1)
    def _():
        out_ref[...] = x1_ref[...] + acc_ref[...]


def post_ffn(x, m, wo, g, w_in, w_out, tm, th, name):
    n, d = x.shape
    nh = FFN_HIDDEN // th
    return pl.pallas_call(
        _post_ffn_body,
        grid=(n // tm, nh),
        in_specs=[pl.BlockSpec((tm, d), lambda i, j: (i, 0)),
                  pl.BlockSpec((tm, m.shape[1]), lambda i, j: (i, 0)),
                  pl.BlockSpec(wo.shape, lambda i, j: (0, 0)),
                  pl.BlockSpec((1, d), lambda i, j: (0, 0)),
                  pl.BlockSpec((d, th), lambda i, j: (0, j)),
                  pl.BlockSpec((d, th), lambda i, j: (0, j + nh)),
                  pl.BlockSpec((th, d), lambda i, j: (j, 0))],
        out_specs=pl.BlockSpec((tm, d), lambda i, j: (i, 0)),
        out_shape=jax.ShapeDtypeStruct((n, d), F32),
        scratch_shapes=[pltpu.VMEM((tm, d), F32), pltpu.VMEM((tm, d), BF16), pltpu.VMEM((tm, d), F32)],
        compiler_params=_params(2),
        name=name,
    )(x, m, wo, g, w_in, w_in, w_out)


def _head_pair_norm(y, gain, lo):
    y2 = y * y
    s_lo = jnp.sum(jnp.where(lo, y2, 0.0), axis=-1, keepdims=True)
    s_hi = jnp.sum(jnp.where(lo, 0.0, y2), axis=-1, keepdims=True)
    ms = jnp.where(lo, s_lo, s_hi) * (1.0 / HEAD_DIM)
    return y * lax.rsqrt(ms + NORM_EPS) * gain


def _top_blocks(gate, axis):
    nb = gate.shape[axis]
    idx = lax.broadcasted_iota(jnp.int32, gate.shape, axis)
    rank = jnp.zeros(gate.shape, jnp.int32)
    for m in range(nb):
        c = lax.slice_in_dim(gate, m, m + 1, axis=axis)
        beats = (c > gate) | ((c == gate) & (idx > m))
        rank = rank + beats.astype(jnp.int32)
    return rank < MOBA_TOPK


def _moba_prompt_body(q_ref, k_ref, v_ref, gq_ref, gk_ref, *rest, n_prev):
    if n_prev:
        pk_ref, pv_ref, o_ref, kh_ref, vh_ref, kb_ref, vt_ref, qc_ref, km_ref, s_ref = rest
    else:
        o_ref, kh_ref, vh_ref, kb_ref, vt_ref, qc_ref, km_ref, s_ref = rest
    i = pl.program_id(2)
    nb = k_ref.shape[0] // MOBA_BLOCK
    tq = MOBA_BLOCK
    lo = lax.broadcasted_iota(jnp.int32, (1, LANES), 1) < HEAD_DIM

    @pl.when(i == 0)
    def _():
        kn = _head_pair_norm(k_ref[...], gk_ref[...], lo)
        qn = _head_pair_norm(q_ref[...], gq_ref[...], lo)
        kt = kn.T
        vt = v_ref[...].T
        qt = (qn * (HEAD_DIM ** -0.5 * LOG2E)).T.astype(BF16)
        kh_ref[n_prev] = kt
        vh_ref[n_prev] = vt
        if n_prev:
            kh_ref[:n_prev] = pk_ref[...]
            vh_ref[:n_prev] = pv_ref[...]
        km_ref[...] = jnp.sum(kn.reshape(nb, MOBA_BLOCK, LANES), axis=1) * (1.0 / MOBA_BLOCK)
        dim = lax.broadcasted_iota(jnp.int32, (LANES, 1), 0)
        zero = jnp.zeros_like(qt)
        q_lo, q_hi = jnp.where(dim < HEAD_DIM, qt, zero), jnp.where(dim >= HEAD_DIM, qt, zero)
        for n in range(nb):
            span = slice(n * MOBA_BLOCK, (n + 1) * MOBA_BLOCK)
            kb_ref[n] = kn[span].astype(BF16)
            vt_ref[n] = vt[:, span].astype(BF16)
            qc_ref[n] = jnp.concatenate([q_lo[:, span], q_hi[:, span]], axis=1)

    qcat = qc_ref[i]
    kmb = km_ref[...].astype(BF16)
    blk = lax.broadcasted_iota(jnp.int32, (nb, 2 * tq), 0)
    past = blk < i
    gate = jnp.dot(kmb, qcat, preferred_element_type=F32)
    gate = jnp.where(past, gate, NEG_INF)
    sel = jnp.logical_and(_top_blocks(gate, 0), past).astype(F32)

    key = lax.broadcasted_iota(jnp.int32, (MOBA_BLOCK, 2 * tq), 0)
    qry = jnp.bitwise_and(lax.broadcasted_iota(jnp.int32, (MOBA_BLOCK, 2 * tq), 1), tq - 1)
    s_own = jnp.dot(kb_ref[i], qcat, preferred_element_type=F32)
    s_own = jnp.where(key <= qry, s_own, NEG_INF)
    m_own = jnp.max(s_own, axis=0, keepdims=True)

    def attend(n_blocks):
        m = m_own
        for n in range(n_blocks):
            rows = slice(n * MOBA_BLOCK, (n + 1) * MOBA_BLOCK)
            s = jnp.dot(kb_ref[n], qcat, preferred_element_type=F32)
            s = jnp.where(sel[n:n + 1, :] > 0.5, s, NEG_INF)
            s_ref[rows, :] = s
            m = jnp.maximum(m, jnp.max(s, axis=0, keepdims=True))
        p = jnp.exp2(s_own - m)
        l = jnp.sum(p, axis=0, keepdims=True)
        acc = jnp.dot(vt_ref[i], p.astype(BF16), preferred_element_type=F32)
        for n in range(n_blocks):
            rows = slice(n * MOBA_BLOCK, (n + 1) * MOBA_BLOCK)
            p = jnp.exp2(s_ref[rows, :] - m)
            l = l + jnp.sum(p, axis=0, keepdims=True)
            acc = acc + jnp.dot(vt_ref[n], p.astype(BF16), preferred_element_type=F32)
        o = acc / l
        o_ref[...] = jnp.concatenate([o[:HEAD_DIM, :tq], o[HEAD_DIM:, tq:]], axis=0).T.astype(o_ref.dtype)

    for n_blocks in range(nb):
        pl.when(i == n_blocks)(functools.partial(attend, n_blocks))


def moba_prompt(q, k, v, gq, gk, batch, seq, name, prev_k=None, prev_v=None):
    n = batch * seq
    nb = seq // MOBA_BLOCK
    n_prev = 0 if prev_k is None else prev_k.shape[0]
    seq_rows = lambda b, h, i: (b, h)
    hist = lambda layers: pl.BlockSpec((layers, None, LANES, seq), lambda b, h, i: (0, b, h, 0))
    in_specs = [pl.BlockSpec((seq, LANES), seq_rows),
                pl.BlockSpec((seq, LANES), seq_rows),
                pl.BlockSpec((seq, LANES), seq_rows),
                pl.BlockSpec((1, LANES), lambda b, h, i: (0, 0)),
                pl.BlockSpec((1, LANES), lambda b, h, i: (0, 0))]
    hist_shape = jax.ShapeDtypeStruct((n_prev + 1, batch, D_MODEL, seq), F32)
    out_specs = [pl.BlockSpec((MOBA_BLOCK, LANES), lambda b, h, i: (b * nb + i, h)), hist(n_prev + 1), hist(n_prev + 1)]
    out_shape = [jax.ShapeDtypeStruct((n, D_MODEL), BF16), hist_shape, hist_shape]
    args = [q, k, v, gq, gk]
    if n_prev:
        in_specs += [hist(n_prev)] * 2
        args += [prev_k, prev_v]
    return pl.pallas_call(
        functools.partial(_moba_prompt_body, n_prev=n_prev),
        grid=(batch, N_SLABS, nb),
        in_specs=in_specs,
        out_specs=out_specs,
        out_shape=out_shape,
        scratch_shapes=[pltpu.VMEM((nb, MOBA_BLOCK, LANES), BF16),
                        pltpu.VMEM((nb, LANES, MOBA_BLOCK), BF16),
                        pltpu.VMEM((nb, LANES, HEADS_PER_SLAB * MOBA_BLOCK), BF16),
                        pltpu.VMEM((nb, LANES), F32),
                        pltpu.VMEM(((nb - 1) * MOBA_BLOCK, HEADS_PER_SLAB * MOBA_BLOCK), F32)],
        compiler_params=_params(3),
        name=name,
    )(*args)


def _moba_sample_body(pt_ref, q_ref, k_ref, v_ref, gq_ref, gk_ref, hm_ref, hsel_ref, *rest, n_pages, n_prev):
    del pt_ref
    k_pages, v_pages = rest[:n_pages], rest[n_pages:2 * n_pages]
    if n_prev:
        pk_ref, pv_ref, o_ref, kh_ref, vh_ref, s_ref = rest[2 * n_pages:]
        kh_ref[:n_prev] = pk_ref[...]
        vh_ref[:n_prev] = pv_ref[...]
        vh_ref[n_prev] = v_ref[...]
    else:
        o_ref, kh_ref, s_ref = rest[2 * n_pages:]
    ppb = MOBA_BLOCK // PAGE_SIZE
    nb = n_pages // ppb
    s_new = q_ref.shape[0]
    rows = ATTN_HEADS * s_new
    lo = lax.broadcasted_iota(jnp.int32, (1, LANES), 1) < HEAD_DIM

    def heads_norm(y, gain):
        slabs = [_head_pair_norm(y[:, c * LANES:(c + 1) * LANES], gain[:, c * LANES:(c + 1) * LANES], lo)
                 for c in range(N_SLABS)]
        return jnp.concatenate(slabs, axis=1)

    qn = heads_norm(q_ref[...], gq_ref[...])
    kn = heads_norm(k_ref[...], gk_ref[...])
    kh_ref[n_prev] = kn
    qs = qn * (HEAD_DIM ** -0.5)
    gh, gw = hm_ref.shape[0] // s_new, hm_ref.shape[1]
    gr = gh * s_new
    n_groups = ATTN_HEADS // gh
    hm = hm_ref[...]
    q_groups = [(jnp.broadcast_to(qs[None, :, g * gw:(g + 1) * gw], (gh, s_new, gw)).reshape(gr, gw) * hm).astype(BF16)
                for g in range(n_groups)]

    for p in range(n_pages):
        for g in range(n_groups):
            s_ref[g * gr:(g + 1) * gr, p * PAGE_SIZE:(p + 1) * PAGE_SIZE] = jnp.dot(
                q_groups[g], k_pages[p][g * gw:(g + 1) * gw, :].astype(BF16), preferred_element_type=F32)

    def block_scores(n):
        return s_ref[:, n * MOBA_BLOCK:(n + 1) * MOBA_BLOCK]

    blk = lax.broadcasted_iota(jnp.int32, (rows, nb), 1)
    gate = jnp.zeros((rows, nb), F32)
    for n in range(nb):
        g_n = jnp.sum(block_scores(n), axis=-1, keepdims=True) * (1.0 / MOBA_BLOCK)
        gate = jnp.where(blk == n, g_n, gate)
    sel = _top_blocks(gate, 1)

    knb = kn.astype(BF16)
    s_own = jnp.concatenate(
        [lax.dot_general(q_groups[g], knb[:, g * gw:(g + 1) * gw], _NT, preferred_element_type=F32)
         for g in range(n_groups)], axis=0)
    q_pos = jnp.bitwise_and(lax.broadcasted_iota(jnp.int32, (rows, s_new), 0), s_new - 1)
    k_pos = lax.broadcasted_iota(jnp.int32, (rows, s_new), 1)
    s_own = jnp.where(k_pos <= q_pos, s_own, NEG_INF)

    def masked(n):
        return jnp.where(sel[:, n:n + 1], block_scores(n), NEG_INF)

    m = jnp.max(s_own, axis=-1, keepdims=True)
    for n in range(nb):
        m = jnp.maximum(m, jnp.max(masked(n), axis=-1, keepdims=True))
    p_own = jnp.exp(s_own - m)
    l = jnp.sum(p_own, axis=-1, keepdims=True)
    pob, vb = p_own.astype(BF16), v_ref[...].astype(BF16)
    accs = [jnp.dot(pob[g * gr:(g + 1) * gr], vb[:, g * gw:(g + 1) * gw], preferred_element_type=F32)
            for g in range(n_groups)]
    for n in range(nb):
        p = jnp.exp(masked(n) - m)
        l = l + jnp.sum(p, axis=-1, keepdims=True)
        pb = p.astype(BF16)
        for c in range(ppb):
            page = v_pages[n * ppb + c]
            for g in range(n_groups):
                accs[g] = accs[g] + lax.dot_general(
                    pb[g * gr:(g + 1) * gr, c * PAGE_SIZE:(c + 1) * PAGE_SIZE],
                    page[g * gw:(g + 1) * gw, :].astype(BF16), _NT, preferred_element_type=F32)
    hsel = hsel_ref[...]
    outs = []
    for g in range(n_groups):
        o = accs[g] / l[g * gr:(g + 1) * gr]
        outs.append(jnp.sum(o.reshape(gh, s_new, gw) * hsel[:, None, :], axis=0))
    o_ref[...] = jnp.concatenate(outs, axis=1)


def moba_sample(q, k, v, gq, gk, cache_kt, cache_vt, layer, page_table, name, prev_k=None, prev_v=None):
    dec_batch, n_pages = page_table.shape
    n_prev = 0 if prev_k is None else prev_k.shape[0]
    n = q.shape[0]
    s_new = n // dec_batch
    assert s_new & (s_new - 1) == 0, "new-token count must be a power of two"
    rows = ATTN_HEADS * s_new
    gh = SAMPLE_GROUP_HEADS
    gw = gh * HEAD_DIM
    head_of_lane = jnp.arange(gw) // HEAD_DIM
    hm = (jnp.arange(gh * s_new)[:, None] // s_new == head_of_lane[None, :]).astype(F32)
    hsel = (jnp.arange(gh)[:, None] == head_of_lane[None, :]).astype(F32)

    def page_spec(p):
        return pl.BlockSpec((None, None, D_MODEL, PAGE_SIZE),
                            lambda b, pt: (layer, pt[b * n_pages + p], 0, 0))

    const = lambda b, pt: (0, 0)
    token = pl.BlockSpec((s_new, D_MODEL), lambda b, pt: (b, 0))
    hist = lambda layers: pl.BlockSpec((layers, s_new, D_MODEL), lambda b, pt: (0, b, 0))
    in_specs = ([token, token, token,
                 pl.BlockSpec((1, D_MODEL), const),
                 pl.BlockSpec((1, D_MODEL), const),
                 pl.BlockSpec(hm.shape, const),
                 pl.BlockSpec(hsel.shape, const)]
                + [page_spec(p) for p in range(n_pages)] * 2)
    out_specs = [token, hist(n_prev + 1)]
    out_shape = [jax.ShapeDtypeStruct((n, D_MODEL), F32), jax.ShapeDtypeStruct((n_prev + 1, n, D_MODEL), F32)]
    args = [page_table.reshape(-1), q, k, v, gq, gk, hm, hsel, *([cache_kt] * n_pages), *([cache_vt] * n_pages)]
    if n_prev:
        in_specs += [hist(n_prev)] * 2
        out_specs.append(out_specs[-1])
        out_shape.append(out_shape[-1])
        args += [prev_k, prev_v]
    grid_spec = pltpu.PrefetchScalarGridSpec(
        num_scalar_prefetch=1,
        grid=(dec_batch,),
        in_specs=in_specs,
        out_specs=out_specs,
        scratch_shapes=[pltpu.VMEM((rows, n_pages * PAGE_SIZE), F32)],
    )
    return pl.pallas_call(
        functools.partial(_moba_sample_body, n_pages=n_pages, n_prev=n_prev),
        grid_spec=grid_spec,
        out_shape=out_shape,
        compiler_params=_params(1),
        name=name,
    )(*args)


def _cumsum_rows(tri, a):
    hi = a.astype(BF16)
    r1 = a - hi.astype(F32)
    mid = r1.astype(BF16)
    lo = (r1 - mid.astype(F32)).astype(BF16)
    n = a.shape[1]
    r = jnp.dot(tri, jnp.concatenate([hi, mid, lo], axis=1), preferred_element_type=F32)
    return r[:, :n] + r[:, n:2 * n] + r[:, 2 * n:]


def _boundary_rows(b, w):
    c, d = b.shape
    half = w // 2
    if half >= SUBLANES:
        return jnp.concatenate([jnp.broadcast_to(b[j * w + half - 1:j * w + half], (w, d)) for j in range(c // w)], axis=0)
    b3 = b.reshape(c // SUBLANES, SUBLANES, d)
    sub = lax.broadcasted_iota(jnp.int32, (1, SUBLANES, 1), 1)
    picks = [b3[:, j * w + half - 1:j * w + half, :] for j in range(SUBLANES // w)]
    ref = picks[-1]
    for j in range(SUBLANES // w - 2, -1, -1):
        ref = jnp.where(sub < (j + 1) * w, picks[j], ref)
    return jnp.broadcast_to(ref, b3.shape).reshape(c, d)


def _gla_intra(q, k, b, c):
    heads = [slice(h * GLA_DK, (h + 1) * GLA_DK) for h in range(q.shape[1] // GLA_DK)]
    row = lax.broadcasted_iota(jnp.int32, (c, 1), 0)
    rr = lax.broadcasted_iota(jnp.int32, (c, c), 0)
    cc = lax.broadcasted_iota(jnp.int32, (c, c), 1)
    apart = jnp.bitwise_xor(rr, cc)
    qk = q * k
    a = [jnp.where(rr == cc, jnp.sum(qk[:, h], axis=-1, keepdims=True), 0.0) for h in heads]
    w = c
    while w >= 2:
        d = b - _boundary_rows(b, w)
        f = jnp.exp(jnp.minimum(d, -d))
        lower = jnp.bitwise_and(row, w // 2) != 0
        qt = jnp.where(lower, q * f, 0.0).astype(BF16)
        kt = jnp.where(lower, 0.0, k * f).astype(BF16)
        for i, h in enumerate(heads):
            a_w = lax.dot_general(qt[:, h], kt[:, h], _NT, preferred_element_type=F32)
            a[i] = a[i] + (a_w if w == c else jnp.where(apart < w, a_w, 0.0))
        w //= 2
    return a


def _gla_body(q_ref, k_ref, v_ref, r_ref, gl_ref, wg_ref, bg_ref, tri_ref, gn_ref, s0_ref, *rest,
              c, chunks_per_seq, n_prev):
    if n_prev:
        ps_ref, o_ref, so_ref, s_ref = rest
    else:
        o_ref, so_ref, s_ref = rest
    t = pl.program_id(1)

    @pl.when(t == 0)
    def _():
        s_ref[...] = s0_ref[...]

    tri = tri_ref[...]
    scale = GLA_DK ** -0.5

    n_seqs = s0_ref.shape[0]

    def chunk(ci, carry):
        rows = pl.ds(pl.multiple_of(ci * c, c), c)
        u = ci if chunks_per_seq == 1 else lax.div(ci, chunks_per_seq)
        x = jnp.dot(gl_ref[rows, :].astype(BF16), wg_ref[...], preferred_element_type=F32) + bg_ref[...]
        log_a = (jnp.minimum(x, 0.0) - jnp.log1p(jnp.exp(-jnp.abs(x)))) * (1.0 / GLA_GATE_TAU)
        b = _cumsum_rows(tri, log_a)
        q = q_ref[rows, :] * scale
        k = k_ref[rows, :]
        v = v_ref[rows, :].astype(BF16)
        b_last = b[c - 1:c]
        q_in = (q * jnp.exp(b)).astype(BF16)
        k_out = (k * jnp.exp(b_last - b)).astype(BF16)
        decay = jnp.broadcast_to(jnp.exp(b_last), (SUBLANES, b.shape[1])).T[:, :1]
        a = _gla_intra(q, k, b, c)
        outs = []
        for hd in range(GLA_HEADS):
            kc = slice(hd * GLA_DK, (hd + 1) * GLA_DK)
            vc = slice(hd * GLA_DV, (hd + 1) * GLA_DV)
            state = s_ref[u, hd]
            o = jnp.dot(q_in[:, kc], state.astype(BF16), preferred_element_type=F32)
            o = o + jnp.dot(a[hd].astype(BF16), v[:, vc], preferred_element_type=F32)
            s_ref[u, hd] = state * decay[kc] + lax.dot_general(k_out[:, kc], v[:, vc], _TN, preferred_element_type=F32)
            outs.append(o * _rms_scale(o))
        o_ref[rows, :] = (jnp.concatenate(outs, axis=1) * gn_ref[...] * _silu(r_ref[rows, :])).astype(o_ref.dtype)
        return carry

    lax.fori_loop(0, n_seqs * chunks_per_seq, chunk, 0, unroll=True)

    @pl.when(t == pl.num_programs(1) - 1)
    def _():
        if n_prev:
            so_ref[:n_prev] = ps_ref[...]
        so_ref[n_prev] = s_ref[...]


def gla_mix(proj, w_gate, b_gate, g_out, s0, layer, batch, seq, c, tt, name, prev_states=None, out_dtype=F32,
            seqs=1):
    n = batch * seq
    nt = seq // tt
    assert seqs == 1 or nt == 1
    hk, hv = GLA_HEADS * GLA_DK, GLA_HEADS * GLA_DV
    n_prev = 0 if prev_states is None else prev_states.shape[0]
    tri = jnp.tril(jnp.ones((c, c), F32)).astype(BF16)
    row = lambda b, t: b * nt + t
    state_block = (None, seqs, GLA_HEADS, GLA_DK, GLA_DV)
    hist = lambda layers: pl.BlockSpec((layers, seqs, GLA_HEADS, GLA_DK, GLA_DV), lambda b, t: (0, b, 0, 0, 0))
    tr = seqs * tt
    in_specs = [pl.BlockSpec((tr, hk), lambda b, t: (row(b, t), 0)),
                pl.BlockSpec((tr, hk), lambda b, t: (row(b, t), 1)),
                pl.BlockSpec((tr, hv), lambda b, t: (row(b, t), 1)),
                pl.BlockSpec((tr, hv), lambda b, t: (row(b, t), 2)),
                pl.BlockSpec((tr, LANES), lambda b, t: (row(b, t), (2 * hk + 2 * hv) // LANES)),
                pl.BlockSpec(w_gate.shape, lambda b, t: (0, 0)),
                pl.BlockSpec((1, hk), lambda b, t: (0, 0)),
                pl.BlockSpec((c, c), lambda b, t: (0, 0)),
                pl.BlockSpec((1, hv), lambda b, t: (0, 0)),
                pl.BlockSpec(state_block, lambda b, t: (layer, b, 0, 0, 0))]
    args = [proj, proj, proj, proj, proj, w_gate, b_gate, tri, g_out, s0]
    if n_prev:
        in_specs.append(hist(n_prev))
        args.append(prev_states)
    return pl.pallas_call(
        functools.partial(_gla_body, c=c, chunks_per_seq=tt // c, n_prev=n_prev),
        grid=(batch // seqs, nt),
        in_specs=in_specs,
        out_specs=[pl.BlockSpec((tr, hv), lambda b, t: (row(b, t), 0)), hist(n_prev + 1)],
        out_shape=[jax.ShapeDtypeStruct((n, hv), out_dtype),
                   jax.ShapeDtypeStruct((n_prev + 1, batch, GLA_HEADS, GLA_DK, GLA_DV), F32)],
        scratch_shapes=[pltpu.VMEM((seqs, GLA_HEADS, GLA_DK, GLA_DV), F32)],
        compiler_params=_params(2),
        name=name,
    )(*args)


def kernel(x_prompt, x_sample, cache_k, cache_v, state_gla, page_table, norm_mixer, norm_ffn, w_qkv, q_norm,
           k_norm, w_attn_o, w_gla_in, w_gla_gate, b_gla_gate, gla_norm, w_gla_o, w_ffn_in, w_ffn_out):
    bp, tp, d = x_prompt.shape
    bs, ts, _ = x_sample.shape
    xp = x_prompt.reshape(bp * tp, d)
    xs = x_sample.reshape(bs * ts, d)
    n_layers_attn, n_pool = cache_k.shape[:2]
    ckt = jnp.transpose(cache_k, (0, 1, 3, 4, 2)).reshape(n_layers_attn, n_pool, d, PAGE_SIZE)
    cvt = jnp.transpose(cache_v, (0, 1, 3, 4, 2)).reshape(n_layers_attn, n_pool, d, PAGE_SIZE)
    hk, hv = GLA_HEADS * GLA_DK, GLA_HEADS * GLA_DV
    zero_state = jnp.zeros((1, bp, GLA_HEADS, GLA_DK, GLA_DV), F32)
    tm_p, tm_s = 1024, 512
    head_shape = (ATTN_HEADS, HEAD_DIM)

    kh_p = vh_p = kh_s = vh_s = st_p = st_s = None
    for i in range(DEPTH):
        j = i // 2
        g_mix = norm_mixer[i][None]
        if i % 2 == 0:
            w = w_qkv[j].astype(BF16)
            q_p, k_p, v_p = norm_matmul(xp, g_mix, w, tm_p, [d, d, d], f"qkv_p{i}")
            q_s, k_s, v_s = norm_matmul(xs, g_mix, w, tm_s, [d, d, d], f"qkv_s{i}")
            gq = jnp.tile(q_norm[j], ATTN_HEADS)[None]
            gk = jnp.tile(k_norm[j], ATTN_HEADS)[None]
            mp, kh_p, vh_p = moba_prompt(q_p, k_p, v_p, gq[:, :LANES], gk[:, :LANES], bp, tp, f"moba_p{i}",
                                         kh_p, vh_p)
            if kh_s is None:
                ms, kh_s = moba_sample(q_s, k_s, v_s, gq, gk, ckt, cvt, j, page_table, f"moba_s{i}")
                vh_s = v_s[None]
            else:
                ms, kh_s, vh_s = moba_sample(q_s, k_s, v_s, gq, gk, ckt, cvt, j, page_table, f"moba_s{i}",
                                             kh_s, vh_s)
            wo = w_attn_o[j].astype(BF16)
        else:
            w_main = w_gla_in[j][:, :2 * hk + 2 * hv]
            w_low = jnp.pad(w_gla_in[j][:, 2 * hk + 2 * hv:], ((0, 0), (0, LANES - GLA_GATE_RANK)))
            w = jnp.concatenate([w_main, w_low], axis=1).astype(BF16)
            proj_p, = norm_matmul(xp, g_mix, w, tm_p, [w.shape[1]], f"gla_in_p{i}")
            proj_s, = norm_matmul(xs, g_mix, w, tm_s, [w.shape[1]], f"gla_in_s{i}")
            w_gate = jnp.pad(w_gla_gate[j], ((0, LANES - GLA_GATE_RANK), (0, 0))).astype(BF16)
            b_gate = b_gla_gate[j][None]
            g_out = jnp.tile(gla_norm[j], GLA_HEADS)[None]
            mp, st_p = gla_mix(proj_p, w_gate, b_gate, g_out, zero_state, 0, bp, tp,
                               GLA_CHUNK, 256, f"gla_p{i}", st_p, BF16)
            ms, st_s = gla_mix(proj_s, w_gate, b_gate, g_out, state_gla, j, bs, ts, ts, ts, f"gla_s{i}", st_s,
                               seqs=GLA_SAMPLE_SEQS)
            wo = w_gla_o[j].astype(BF16)
        g_ffn = norm_ffn[i][None]
        w_in = w_ffn_in[i].astype(BF16)
        w_out = w_ffn_out[i].astype(BF16)
        xp = post_ffn(xp, mp, wo, g_ffn, w_in, w_out, tm_p, 256, f"ffn_p{i}")
        xs = post_ffn(xs, ms, wo, g_ffn, w_in, w_out, tm_s, 256, f"ffn_s{i}")
    rows_p = lambda h: jnp.transpose(h.reshape(h.shape[0], bp, *head_shape, tp), (0, 1, 4, 2, 3))
    rows_s = lambda h: h.reshape(h.shape[0], bs, ts, *head_shape)
    return (xp.reshape(bp, tp, d), xs.reshape(bs, ts, d), rows_p(kh_p), rows_p(vh_p), rows_s(kh_s), rows_s(vh_s),
            st_p, st_s)
```

```python
import functools
import math

import jax
import jax.numpy as jnp
from jax import lax
from jax.experimental import pallas as pl
from jax.experimental.pallas import tpu as pltpu

F32, BF16 = jnp.float32, jnp.bfloat16

D_MODEL = 1024
DEPTH = 4
ATTN_HEADS = 16
HEAD_DIM = 64
MOBA_BLOCK = 256
MOBA_TOPK = 3
PAGE_SIZE = 128
GLA_HEADS = 4
GLA_DK = 128
GLA_DV = 256
GLA_GATE_RANK = 16
GLA_GATE_TAU = 16.0
GLA_CHUNK = 64
FFN_HIDDEN = 2816
FFN_CHUNK = 256
NORM_EPS = 1e-6
NEG_INF = -1e30
LOG2E = math.log2(math.e)

LANES = 128
SUBLANES = 8
VMEM_LIMIT_BYTES = 56 * 1024 * 1024

HEADS_PER_SLAB = LANES // HEAD_DIM
N_SLABS = ATTN_HEADS // HEADS_PER_SLAB
GLA_SAMPLE_SEQS = 4
SAMPLE_GROUP_HEADS = 4

_NT = (((1,), (1,)), ((), ()))
_TN = (((0,), (0,)), ((), ()))


def _params(n_axes):
    return pltpu.CompilerParams(dimension_semantics=("arbitrary",) * n_axes,
                                vmem_limit_bytes=VMEM_LIMIT_BYTES)


def _rms_scale(x):
    return lax.rsqrt(jnp.mean(x * x, axis=-1, keepdims=True) + NORM_EPS)


def _silu(x):
    return x / (1.0 + jnp.exp(-x))


def _norm_matmul_body(x_ref, g_ref, w_ref, *o_refs):
    x = x_ref[...]
    h = (x * _rms_scale(x) * g_ref[...]).astype(BF16)
    col = 0
    for o_ref in o_refs:
        width = o_ref.shape[1]
        o_ref[...] = jnp.dot(h, w_ref[:, col:col + width], preferred_element_type=F32)
        col += width


def norm_matmul(x, g, w, tm, widths, name):
    n, d = x.shape
    assert sum(widths) == w.shape[1]
    return pl.pallas_call(
        _norm_matmul_body,
        grid=(n // tm,),
        in_specs=[pl.BlockSpec((tm, d), lambda i: (i, 0)),
                  pl.BlockSpec((1, d), lambda i: (0, 0)),
                  pl.BlockSpec(w.shape, lambda i: (0, 0))],
        out_specs=[pl.BlockSpec((tm, width), lambda i: (i, 0)) for width in widths],
        out_shape=[jax.ShapeDtypeStruct((n, width), F32) for width in widths],
        compiler_params=_params(1),
        name=name,
    )(x, g, w)


def _post_ffn_body(x_ref, m_ref, wo_ref, g_ref, wi_ref, wd_ref, out_ref):
    x1 = x_ref[...] + jnp.dot(m_ref[...].astype(BF16), wo_ref[...], preferred_element_type=F32)
    h = (x1 * _rms_scale(x1) * g_ref[...]).astype(BF16)
    out_ref[...] = x1
    nh = wd_ref.shape[0]

    def hidden_chunk(j, carry):
        gate = jnp.dot(h, wi_ref[j], preferred_element_type=F32)
        up = jnp.dot(h, wi_ref[nh + j], preferred_element_type=F32)
        act = (_silu(gate) * up).astype(BF16)
        out_ref[...] += jnp.dot(act, wd_ref[j], preferred_element_type=F32)
        return carry

    lax.fori_loop(0, nh, hidden_chunk, 0)


def post_ffn(x, m, wo, g, w_in, w_out, tm, name):
    n, d = x.shape
    resident = lambda a: pl.BlockSpec(a.shape, lambda i: (0,) * a.ndim, pipeline_mode=pl.Buffered(1))
    return pl.pallas_call(
        _post_ffn_body,
        grid=(n // tm,),
        in_specs=[pl.BlockSpec((tm, d), lambda i: (i, 0)),
                  pl.BlockSpec((tm, m.shape[1]), lambda i: (i, 0)),
                  resident(wo),
                  pl.BlockSpec((1, d), lambda i: (0, 0)),
                  resident(w_in),
                  resident(w_out)],
        out_specs=pl.BlockSpec((tm, d), lambda i: (i, 0)),
        out_shape=jax.ShapeDtypeStruct((n, d), F32),
        compiler_params=_params(1),
        name=name,
    )(x, m, wo, g, w_in, w_out)


def _head_pair_norm(y, gain, lo):
    y2 = y * y
    s_lo = jnp.sum(jnp.where(lo, y2, 0.0), axis=-1, keepdims=True)
    s_hi = jnp.sum(jnp.where(lo, 0.0, y2), axis=-1, keepdims=True)
    ms = jnp.where(lo, s_lo, s_hi) * (1.0 / HEAD_DIM)
    return y * lax.rsqrt(ms + NORM_EPS) * gain


def _top_blocks(gate, axis):
    nb = gate.shape[axis]
    idx = lax.broadcasted_iota(jnp.int32, gate.shape, axis)
    rank = jnp.zeros(gate.shape, jnp.int32)
    for m in range(nb):
        c = lax.slice_in_dim(gate, m, m + 1, axis=axis)
        beats = (c > gate) | ((c == gate) & (idx > m))
        rank = rank + beats.astype(jnp.int32)
    return rank < MOBA_TOPK


def _moba_prompt_body(q_ref, k_ref, v_ref, gq_ref, gk_ref, *rest, n_prev):
    if n_prev:
        pk_ref, pv_ref, o_ref, kh_ref, vh_ref, kb_ref, vt_ref, qc_ref, s_ref = rest
    else:
        o_ref, kh_ref, vh_ref, kb_ref, vt_ref, qc_ref, s_ref = rest
    nb = k_ref.shape[0] // MOBA_BLOCK
    tq = MOBA_BLOCK
    lo = lax.broadcasted_iota(jnp.int32, (1, LANES), 1) < HEAD_DIM

    kn = _head_pair_norm(k_ref[...], gk_ref[...], lo)
    qn = _head_pair_norm(q_ref[...], gq_ref[...], lo)
    kt = kn.T
    vt = v_ref[...].T
    qt = (qn * (HEAD_DIM ** -0.5 * LOG2E)).T.astype(BF16)
    kh_ref[n_prev] = kt
    vh_ref[n_prev] = vt
    if n_prev:
        kh_ref[:n_prev] = pk_ref[...]
        vh_ref[:n_prev] = pv_ref[...]
    kmb = (jnp.sum(kn.reshape(nb, MOBA_BLOCK, LANES), axis=1) * (1.0 / MOBA_BLOCK)).astype(BF16)
    dim = lax.broadcasted_iota(jnp.int32, (LANES, 1), 0)
    zero = jnp.zeros_like(qt)
    q_lo, q_hi = jnp.where(dim < HEAD_DIM, qt, zero), jnp.where(dim >= HEAD_DIM, qt, zero)
    for n in range(nb):
        span = slice(n * MOBA_BLOCK, (n + 1) * MOBA_BLOCK)
        kb_ref[n] = kn[span].astype(BF16)
        vt_ref[n] = vt[:, span].astype(BF16)
        qc_ref[n] = jnp.concatenate([q_lo[:, span], q_hi[:, span]], axis=1)

    blk = lax.broadcasted_iota(jnp.int32, (nb, 2 * tq), 0)
    key = lax.broadcasted_iota(jnp.int32, (MOBA_BLOCK, 2 * tq), 0)
    qry = jnp.bitwise_and(lax.broadcasted_iota(jnp.int32, (MOBA_BLOCK, 2 * tq), 1), tq - 1)
    causal = key <= qry

    for i in range(nb):
        qcat = qc_ref[i]
        s_own = jnp.dot(kb_ref[i], qcat, preferred_element_type=F32)
        s_own = jnp.where(causal, s_own, NEG_INF)
        m = jnp.max(s_own, axis=0, keepdims=True)
        if i:
            past = blk < i
            gate = jnp.dot(kmb, qcat, preferred_element_type=F32)
            gate = jnp.where(past, gate, NEG_INF)
            sel = jnp.logical_and(_top_blocks(gate, 0), past).astype(F32)
        scores = s_ref.at[i % 2]
        for n in range(i):
            rows = slice(n * MOBA_BLOCK, (n + 1) * MOBA_BLOCK)
            s = jnp.dot(kb_ref[n], qcat, preferred_element_type=F32)
            s = jnp.where(sel[n:n + 1, :] > 0.5, s, NEG_INF)
            scores[rows, :] = s
            m = jnp.maximum(m, jnp.max(s, axis=0, keepdims=True))
        p = jnp.exp2(s_own - m)
        l = jnp.sum(p, axis=0, keepdims=True)
        acc = jnp.dot(vt_ref[i], p.astype(BF16), preferred_element_type=F32)
        for n in range(i):
            rows = slice(n * MOBA_BLOCK, (n + 1) * MOBA_BLOCK)
            p = jnp.exp2(scores[rows, :] - m)
            l = l + jnp.sum(p, axis=0, keepdims=True)
            acc = acc + jnp.dot(vt_ref[n], p.astype(BF16), preferred_element_type=F32)
        o = acc / l
        o_ref[i * tq:(i + 1) * tq, :] = jnp.concatenate([o[:HEAD_DIM, :tq], o[HEAD_DIM:, tq:]],
                                                        axis=0).T.astype(o_ref.dtype)


def moba_prompt(q, k, v, gq, gk, batch, seq, name, prev_k=None, prev_v=None):
    n = batch * seq
    nb = seq // MOBA_BLOCK
    n_prev = 0 if prev_k is None else prev_k.shape[0]
    rows = pl.BlockSpec((seq, LANES), lambda b, h: (b, h))
    gain = pl.BlockSpec((1, LANES), lambda b, h: (0, 0))
    hist = lambda layers: pl.BlockSpec((layers, None, LANES, seq), lambda b, h: (0, b, h, 0))
    in_specs = [rows, rows, rows, gain, gain]
    hist_shape = jax.ShapeDtypeStruct((n_prev + 1, batch, D_MODEL, seq), F32)
    out_specs = [rows, hist(n_prev + 1), hist(n_prev + 1)]
    out_shape = [jax.ShapeDtypeStruct((n, D_MODEL), BF16), hist_shape, hist_shape]
    args = [q, k, v, gq, gk]
    if n_prev:
        in_specs += [hist(n_prev)] * 2
        args += [prev_k, prev_v]
    return pl.pallas_call(
        functools.partial(_moba_prompt_body, n_prev=n_prev),
        grid=(batch, N_SLABS),
        in_specs=in_specs,
        out_specs=out_specs,
        out_shape=out_shape,
        scratch_shapes=[pltpu.VMEM((nb, MOBA_BLOCK, LANES), BF16),
                        pltpu.VMEM((nb, LANES, MOBA_BLOCK), BF16),
                        pltpu.VMEM((nb, LANES, HEADS_PER_SLAB * MOBA_BLOCK), BF16),
                        pltpu.VMEM((2, (nb - 1) * MOBA_BLOCK, HEADS_PER_SLAB * MOBA_BLOCK), F32)],
        compiler_params=_params(2),
        name=name,
    )(*args)


def _moba_sample_body(pt_ref, q_ref, k_ref, v_ref, gq_ref, gk_ref, hm_ref, hsel_ref, *rest, n_pages, n_prev):
    del pt_ref
    k_pages, v_pages = rest[:n_pages], rest[n_pages:2 * n_pages]
    if n_prev:
        pk_ref, pv_ref, o_ref, kh_ref, vh_ref, s_ref = rest[2 * n_pages:]
        kh_ref[:n_prev] = pk_ref[...]
        vh_ref[:n_prev] = pv_ref[...]
        vh_ref[n_prev] = v_ref[...]
    else:
        o_ref, kh_ref, s_ref = rest[2 * n_pages:]
    ppb = MOBA_BLOCK // PAGE_SIZE
    nb = n_pages // ppb
    s_new = q_ref.shape[0]
    rows = ATTN_HEADS * s_new
    lo = lax.broadcasted_iota(jnp.int32, (1, LANES), 1) < HEAD_DIM

    def heads_norm(y, gain):
        slabs = [_head_pair_norm(y[:, c * LANES:(c + 1) * LANES], gain[:, c * LANES:(c + 1) * LANES], lo)
                 for c in range(N_SLABS)]
        return jnp.concatenate(slabs, axis=1)

    qn = heads_norm(q_ref[...], gq_ref[...])
    kn = heads_norm(k_ref[...], gk_ref[...])
    kh_ref[n_prev] = kn
    qs = qn * (HEAD_DIM ** -0.5)
    gh, gw = hm_ref.shape[0] // s_new, hm_ref.shape[1]
    gr = gh * s_new
    n_groups = ATTN_HEADS // gh
    hm = hm_ref[...]
    q_groups = [(jnp.broadcast_to(qs[None, :, g * gw:(g + 1) * gw], (gh, s_new, gw)).reshape(gr, gw) * hm).astype(BF16)
                for g in range(n_groups)]

    for p in range(n_pages):
        for g in range(n_groups):
            s_ref[g * gr:(g + 1) * gr, p * PAGE_SIZE:(p + 1) * PAGE_SIZE] = jnp.dot(
                q_groups[g], k_pages[p][g * gw:(g + 1) * gw, :].astype(BF16), preferred_element_type=F32)

    def block_scores(n):
        return s_ref[:, n * MOBA_BLOCK:(n + 1) * MOBA_BLOCK]

    blk = lax.broadcasted_iota(jnp.int32, (rows, nb), 1)
    gate = jnp.zeros((rows, nb), F32)
    for n in range(nb):
        g_n = jnp.sum(block_scores(n), axis=-1, keepdims=True) * (1.0 / MOBA_BLOCK)
        gate = jnp.where(blk == n, g_n, gate)
    sel = _top_blocks(gate, 1)

    knb = kn.astype(BF16)
    s_own = jnp.concatenate(
        [lax.dot_general(q_groups[g], knb[:, g * gw:(g + 1) * gw], _NT, preferred_element_type=F32)
         for g in range(n_groups)], axis=0)
    q_pos = jnp.bitwise_and(lax.broadcasted_iota(jnp.int32, (rows, s_new), 0), s_new - 1)
    k_pos = lax.broadcasted_iota(jnp.int32, (rows, s_new), 1)
    s_own = jnp.where(k_pos <= q_pos, s_own, NEG_INF)

    def masked(n):
        return jnp.where(sel[:, n:n + 1], block_scores(n), NEG_INF)

    m = jnp.max(s_own, axis=-1, keepdims=True)
    for n in range(nb):
        m = jnp.maximum(m, jnp.max(masked(n), axis=-1, keepdims=True))
    p_own = jnp.exp(s_own - m)
    l = jnp.sum(p_own, axis=-1, keepdims=True)
    pob, vb = p_own.astype(BF16), v_ref[...].astype(BF16)
    accs = [jnp.dot(pob[g * gr:(g + 1) * gr], vb[:, g * gw:(g + 1) * gw], preferred_element_type=F32)
            for g in range(n_groups)]
    for n in range(nb):
        p = jnp.exp(masked(n) - m)
        l = l + jnp.sum(p, axis=-1, keepdims=True)
        pb = p.astype(BF16)
        for c in range(ppb):
            page = v_pages[n * ppb + c]
            for g in range(n_groups):
                accs[g] = accs[g] + lax.dot_general(
                    pb[g * gr:(g + 1) * gr, c * PAGE_SIZE:(c + 1) * PAGE_SIZE],
                    page[g * gw:(g + 1) * gw, :].astype(BF16), _NT, preferred_element_type=F32)
    hsel = hsel_ref[...]
    outs = []
    for g in range(n_groups):
        o = accs[g] / l[g * gr:(g + 1) * gr]
        outs.append(jnp.sum(o.reshape(gh, s_new, gw) * hsel[:, None, :], axis=0))
    o_ref[...] = jnp.concatenate(outs, axis=1)


def moba_sample(q, k, v, gq, gk, cache_kt, cache_vt, layer, page_table, name, prev_k=None, prev_v=None):
    dec_batch, n_pages = page_table.shape
    n_prev = 0 if prev_k is None else prev_k.shape[0]
    n = q.shape[0]
    s_new = n // dec_batch
    assert s_new & (s_new - 1) == 0, "new-token count must be a power of two"
    rows = ATTN_HEADS * s_new
    gh = SAMPLE_GROUP_HEADS
    gw = gh * HEAD_DIM
    head_of_lane = jnp.arange(gw) // HEAD_DIM
    hm = (jnp.arange(gh * s_new)[:, None] // s_new == head_of_lane[None, :]).astype(F32)
    hsel = (jnp.arange(gh)[:, None] == head_of_lane[None, :]).astype(F32)

    def page_spec(p):
        return pl.BlockSpec((None, None, D_MODEL, PAGE_SIZE),
                            lambda b, pt: (layer, pt[b * n_pages + p], 0, 0))

    const = lambda b, pt: (0, 0)
    token = pl.BlockSpec((s_new, D_MODEL), lambda b, pt: (b, 0))
    hist = lambda layers: pl.BlockSpec((layers, s_new, D_MODEL), lambda b, pt: (0, b, 0))
    in_specs = ([token, token, token,
                 pl.BlockSpec((1, D_MODEL), const),
                 pl.BlockSpec((1, D_MODEL), const),
                 pl.BlockSpec(hm.shape, const),
                 pl.BlockSpec(hsel.shape, const)]
                + [page_spec(p) for p in range(n_pages)] * 2)
    out_specs = [token, hist(n_prev + 1)]
    out_shape = [jax.ShapeDtypeStruct((n, D_MODEL), F32), jax.ShapeDtypeStruct((n_prev + 1, n, D_MODEL), F32)]
    args = [page_table.reshape(-1), q, k, v, gq, gk, hm, hsel, *([cache_kt] * n_pages), *([cache_vt] * n_pages)]
    if n_prev:
        in_specs += [hist(n_prev)] * 2
        out_specs.append(out_specs[-1])
        out_shape.append(out_shape[-1])
        args += [prev_k, prev_v]
    grid_spec = pltpu.PrefetchScalarGridSpec(
        num_scalar_prefetch=1,
        grid=(dec_batch,),
        in_specs=in_specs,
        out_specs=out_specs,
        scratch_shapes=[pltpu.VMEM((rows, n_pages * PAGE_SIZE), F32)],
    )
    return pl.pallas_call(
        functools.partial(_moba_sample_body, n_pages=n_pages, n_prev=n_prev),
        grid_spec=grid_spec,
        out_shape=out_shape,
        compiler_params=_params(1),
        name=name,
    )(*args)


def _cumsum_rows(tri, a):
    hi = a.astype(BF16)
    r1 = a - hi.astype(F32)
    mid = r1.astype(BF16)
    lo = (r1 - mid.astype(F32)).astype(BF16)
    n = a.shape[1]
    r = jnp.dot(tri, jnp.concatenate([hi, mid, lo], axis=1), preferred_element_type=F32)
    return r[:, :n] + r[:, n:2 * n] + r[:, 2 * n:]


def _boundary_rows(b, w):
    c, d = b.shape
    half = w // 2
    if half >= SUBLANES:
        return jnp.concatenate([jnp.broadcast_to(b[j * w + half - 1:j * w + half], (w, d)) for j in range(c // w)], axis=0)
    b3 = b.reshape(c // SUBLANES, SUBLANES, d)
    sub = lax.broadcasted_iota(jnp.int32, (1, SUBLANES, 1), 1)
    picks = [b3[:, j * w + half - 1:j * w + half, :] for j in range(SUBLANES // w)]
    ref = picks[-1]
    for j in range(SUBLANES // w - 2, -1, -1):
        ref = jnp.where(sub < (j + 1) * w, picks[j], ref)
    return jnp.broadcast_to(ref, b3.shape).reshape(c, d)


def _gla_intra(q, k, b, c):
    heads = [slice(h * GLA_DK, (h + 1) * GLA_DK) for h in range(q.shape[1] // GLA_DK)]
    row = lax.broadcasted_iota(jnp.int32, (c, 1), 0)
    rr = lax.broadcasted_iota(jnp.int32, (c, c), 0)
    cc = lax.broadcasted_iota(jnp.int32, (c, c), 1)
    apart = jnp.bitwise_xor(rr, cc)
    qk = q * k
    a = [jnp.where(rr == cc, jnp.sum(qk[:, h], axis=-1, keepdims=True), 0.0) for h in heads]
    w = c
    while w >= 2:
        d = b - _boundary_rows(b, w)
        f = jnp.exp(jnp.minimum(d, -d))
        lower = jnp.bitwise_and(row, w // 2) != 0
        qt = jnp.where(lower, q * f, 0.0).astype(BF16)
        kt = jnp.where(lower, 0.0, k * f).astype(BF16)
        for i, h in enumerate(heads):
            a_w = lax.dot_general(qt[:, h], kt[:, h], _NT, preferred_element_type=F32)
            a[i] = a[i] + (a_w if w == c else jnp.where(apart < w, a_w, 0.0))
        w //= 2
    return a


def _gla_body(q_ref, k_ref, v_ref, r_ref, gl_ref, wg_ref, bg_ref, tri_ref, gn_ref, s0_ref, *rest,
              c, chunks_per_seq, n_prev):
    if n_prev:
        ps_ref, o_ref, so_ref, s_ref = rest
    else:
        o_ref, so_ref, s_ref = rest
    t = pl.program_id(1)

    @pl.when(t == 0)
    def _():
        s_ref[...] = s0_ref[...]

    tri = tri_ref[...]
    scale = GLA_DK ** -0.5

    n_seqs = s0_ref.shape[0]

    def chunk(ci, carry):
        rows = pl.ds(pl.multiple_of(ci * c, c), c)
        u = ci if chunks_per_seq == 1 else lax.div(ci, chunks_per_seq)
        x = jnp.dot(gl_ref[rows, :].astype(BF16), wg_ref[...], preferred_element_type=F32) + bg_ref[...]
        log_a = (jnp.minimum(x, 0.0) - jnp.log1p(jnp.exp(-jnp.abs(x)))) * (1.0 / GLA_GATE_TAU)
        b = _cumsum_rows(tri, log_a)
        q = q_ref[rows, :] * scale
        k = k_ref[rows, :]
        v = v_ref[rows, :].astype(BF16)
        b_last = b[c - 1:c]
        q_in = (q * jnp.exp(b)).astype(BF16)
        k_out = (k * jnp.exp(b_last - b)).astype(BF16)
        decay = jnp.broadcast_to(jnp.exp(b_last), (SUBLANES, b.shape[1])).T[:, :1]
        a = _gla_intra(q, k, b, c)
        outs = []
        for hd in range(GLA_HEADS):
            kc = slice(hd * GLA_DK, (hd + 1) * GLA_DK)
            vc = slice(hd * GLA_DV, (hd + 1) * GLA_DV)
            state = s_ref[u, hd]
            o = jnp.dot(q_in[:, kc], state.astype(BF16), preferred_element_type=F32)
            o = o + jnp.dot(a[hd].astype(BF16), v[:, vc], preferred_element_type=F32)
            s_ref[u, hd] = state * decay[kc] + lax.dot_general(k_out[:, kc], v[:, vc], _TN, preferred_element_type=F32)
            outs.append(o * _rms_scale(o))
        o_ref[rows, :] = (jnp.concatenate(outs, axis=1) * gn_ref[...] * _silu(r_ref[rows, :])).astype(o_ref.dtype)
        return carry

    lax.fori_loop(0, n_seqs * chunks_per_seq, chunk, 0, unroll=True)

    @pl.when(t == pl.num_programs(1) - 1)
    def _():
        if n_prev:
            so_ref[:n_prev] = ps_ref[...]
        so_ref[n_prev] = s_ref[...]


def gla_mix(proj, w_gate, b_gate, g_out, s0, layer, batch, seq, c, tt, name, prev_states=None, out_dtype=F32,
            seqs=1):
    n = batch * seq
    nt = seq // tt
    assert seqs == 1 or nt == 1
    hk, hv = GLA_HEADS * GLA_DK, GLA_HEADS * GLA_DV
    n_prev = 0 if prev_states is None else prev_states.shape[0]
    tri = jnp.tril(jnp.ones((c, c), F32)).astype(BF16)
    row = lambda b, t: b * nt + t
    state_block = (None, seqs, GLA_HEADS, GLA_DK, GLA_DV)
    hist = lambda layers: pl.BlockSpec((layers, seqs, GLA_HEADS, GLA_DK, GLA_DV), lambda b, t: (0, b, 0, 0, 0))
    tr = seqs * tt
    in_specs = [pl.BlockSpec((tr, hk), lambda b, t: (row(b, t), 0)),
                pl.BlockSpec((tr, hk), lambda b, t: (row(b, t), 1)),
                pl.BlockSpec((tr, hv), lambda b, t: (row(b, t), 1)),
                pl.BlockSpec((tr, hv), lambda b, t: (row(b, t), 2)),
                pl.BlockSpec((tr, LANES), lambda b, t: (row(b, t), (2 * hk + 2 * hv) // LANES)),
                pl.BlockSpec(w_gate.shape, lambda b, t: (0, 0)),
                pl.BlockSpec((1, hk), lambda b, t: (0, 0)),
                pl.BlockSpec((c, c), lambda b, t: (0, 0)),
                pl.BlockSpec((1, hv), lambda b, t: (0, 0)),
                pl.BlockSpec(state_block, lambda b, t: (layer, b, 0, 0, 0))]
    args = [proj, proj, proj, proj, proj, w_gate, b_gate, tri, g_out, s0]
    if n_prev:
        in_specs.append(hist(n_prev))
        args.append(prev_states)
    return pl.pallas_call(
        functools.partial(_gla_body, c=c, chunks_per_seq=tt // c, n_prev=n_prev),
        grid=(batch // seqs, nt),
        in_specs=in_specs,
        out_specs=[pl.BlockSpec((tr, hv), lambda b, t: (row(b, t), 0)), hist(n_prev + 1)],
        out_shape=[jax.ShapeDtypeStruct((n, hv), out_dtype),
                   jax.ShapeDtypeStruct((n_prev + 1, batch, GLA_HEADS, GLA_DK, GLA_DV), F32)],
        scratch_shapes=[pltpu.VMEM((seqs, GLA_HEADS, GLA_DK, GLA_DV), F32)],
        compiler_params=_params(2),
        name=name,
    )(*args)


def kernel(x_prompt, x_sample, cache_k, cache_v, state_gla, page_table, norm_mixer, norm_ffn, w_qkv, q_norm,
           k_norm, w_attn_o, w_gla_in, w_gla_gate, b_gla_gate, gla_norm, w_gla_o, w_ffn_in, w_ffn_out):
    bp, tp, d = x_prompt.shape
    bs, ts, _ = x_sample.shape
    xp = x_prompt.reshape(bp * tp, d)
    xs = x_sample.reshape(bs * ts, d)
    n_layers_attn, n_pool = cache_k.shape[:2]
    ckt = jnp.transpose(cache_k, (0, 1, 3, 4, 2)).reshape(n_layers_attn, n_pool, d, PAGE_SIZE)
    cvt = jnp.transpose(cache_v, (0, 1, 3, 4, 2)).reshape(n_layers_attn, n_pool, d, PAGE_SIZE)
    hk, hv = GLA_HEADS * GLA_DK, GLA_HEADS * GLA_DV
    zero_state = jnp.zeros((1, bp, GLA_HEADS, GLA_DK, GLA_DV), F32)
    tm_p, tm_s = 1024, 512
    head_shape = (ATTN_HEADS, HEAD_DIM)

    kh_p = vh_p = kh_s = vh_s = st_p = st_s = None
    for i in range(DEPTH):
        j = i // 2
        g_mix = norm_mixer[i][None]
        if i % 2 == 0:
            w = w_qkv[j].astype(BF16)
            q_p, k_p, v_p = norm_matmul(xp, g_mix, w, tm_p, [d, d, d], f"qkv_p{i}")
            q_s, k_s, v_s = norm_matmul(xs, g_mix, w, tm_s, [d, d, d], f"qkv_s{i}")
            gq = jnp.tile(q_norm[j], ATTN_HEADS)[None]
            gk = jnp.tile(k_norm[j], ATTN_HEADS)[None]
            mp, kh_p, vh_p = moba_prompt(q_p, k_p, v_p, gq[:, :LANES], gk[:, :LANES], bp, tp, f"moba_p{i}",
                                         kh_p, vh_p)
            if kh_s is None:
                ms, kh_s = moba_sample(q_s, k_s, v_s, gq, gk, ckt, cvt, j, page_table, f"moba_s{i}")
                vh_s = v_s[None]
            else:
                ms, kh_s, vh_s = moba_sample(q_s, k_s, v_s, gq, gk, ckt, cvt, j, page_table, f"moba_s{i}",
                                             kh_s, vh_s)
            wo = w_attn_o[j].astype(BF16)
        else:
            w_main = w_gla_in[j][:, :2 * hk + 2 * hv]
            w_low = jnp.pad(w_gla_in[j][:, 2 * hk + 2 * hv:], ((0, 0), (0, LANES - GLA_GATE_RANK)))
            w = jnp.concatenate([w_main, w_low], axis=1).astype(BF16)
            proj_p, = norm_matmul(xp, g_mix, w, tm_p, [w.shape[1]], f"gla_in_p{i}")
            proj_s, = norm_matmul(xs, g_mix, w, tm_s, [w.shape[1]], f"gla_in_s{i}")
            w_gate = jnp.pad(w_gla_gate[j], ((0, LANES - GLA_GATE_RANK), (0, 0))).astype(BF16)
            b_gate = b_gla_gate[j][None]
            g_out = jnp.tile(gla_norm[j], GLA_HEADS)[None]
            mp, st_p = gla_mix(proj_p, w_gate, b_gate, g_out, zero_state, 0, bp, tp,
                               GLA_CHUNK, 256, f"gla_p{i}", st_p, BF16)
            ms, st_s = gla_mix(proj_s, w_gate, b_gate, g_out, state_gla, j, bs, ts, ts, ts, f"gla_s{i}", st_s,
                               seqs=GLA_SAMPLE_SEQS)
            wo = w_gla_o[j].astype(BF16)
        g_ffn = norm_ffn[i][None]
        w_in = jnp.transpose(w_ffn_in[i].astype(BF16).reshape(d, -1, FFN_CHUNK), (1, 0, 2))
        w_out = w_ffn_out[i].astype(BF16).reshape(-1, FFN_CHUNK, d)
        xp = post_ffn(xp, mp, wo, g_ffn, w_in, w_out, tm_p, f"ffn_p{i}")
        xs = post_ffn(xs, ms, wo, g_ffn, w_in, w_out, tm_s, f"ffn_s{i}")
    rows_p = lambda h: jnp.transpose(h.reshape(h.shape[0], bp, *head_shape, tp), (0, 1, 4, 2, 3))
    rows_s = lambda h: h.reshape(h.shape[0], bs, ts, *head_shape)
    return (xp.reshape(bp, tp, d), xs.reshape(bs, ts, d), rows_p(kh_p), rows_p(vh_p), rows_s(kh_s), rows_s(vh_s),
            st_p, st_s)
```

```python
import functools
import math

import jax
import jax.numpy as jnp
from jax import lax
from jax.experimental import pallas as pl
from jax.experimental.pallas import tpu as pltpu

F32, BF16 = jnp.float32, jnp.bfloat16

D_MODEL = 1024
DEPTH = 4
ATTN_HEADS = 16
HEAD_DIM = 64
MOBA_BLOCK = 256
MOBA_TOPK = 3
PAGE_SIZE = 128
GLA_HEADS = 4
GLA_DK = 128
GLA_DV = 256
GLA_GATE_RANK = 16
GLA_GATE_TAU = 16.0
GLA_CHUNK = 64
FFN_HIDDEN = 2816
FFN_CHUNK = 256
NORM_EPS = 1e-6
NEG_INF = -1e30
LOG2E = math.log2(math.e)

LANES = 128
SUBLANES = 8
VMEM_LIMIT_BYTES = 56 * 1024 * 1024

HEADS_PER_SLAB = LANES // HEAD_DIM
N_SLABS = ATTN_HEADS // HEADS_PER_SLAB
GLA_SAMPLE_SEQS = 4
SAMPLE_GROUP_HEADS = 4

_NT = (((1,), (1,)), ((), ()))
_TN = (((0,), (0,)), ((), ()))


def _params(n_axes):
    return pltpu.CompilerParams(dimension_semantics=("arbitrary",) * n_axes,
                                vmem_limit_bytes=VMEM_LIMIT_BYTES)


def _rms_scale(x):
    return lax.rsqrt(jnp.mean(x * x, axis=-1, keepdims=True) + NORM_EPS)


def _silu(x):
    return x / (1.0 + jnp.exp(-x))


def _norm_matmul_body(x_ref, g_ref, w_ref, *o_refs):
    x = x_ref[...]
    h = (x * _rms_scale(x) * g_ref[...]).astype(BF16)
    col = 0
    for o_ref in o_refs:
        width = o_ref.shape[1]
        o_ref[...] = jnp.dot(h, w_ref[:, col:col + width], preferred_element_type=F32)
        col += width


def norm_matmul(x, g, w, tm, widths, name):
    n, d = x.shape
    assert sum(widths) == w.shape[1]
    return pl.pallas_call(
        _norm_matmul_body,
        grid=(n // tm,),
        in_specs=[pl.BlockSpec((tm, d), lambda i: (i, 0)),
                  pl.BlockSpec((1, d), lambda i: (0, 0)),
                  pl.BlockSpec(w.shape, lambda i: (0, 0))],
        out_specs=[pl.BlockSpec((tm, width), lambda i: (i, 0)) for width in widths],
        out_shape=[jax.ShapeDtypeStruct((n, width), F32) for width in widths],
        compiler_params=_params(1),
        name=name,
    )(x, g, w)


def _post_ffn_body(x_ref, m_ref, wo_ref, g_ref, wi_ref, wd_ref, out_ref):
    x1 = x_ref[...] + jnp.dot(m_ref[...].astype(BF16), wo_ref[...], preferred_element_type=F32)
    h = (x1 * _rms_scale(x1) * g_ref[...]).astype(BF16)
    out_ref[...] = x1
    hidden = wd_ref.shape[0]

    def hidden_chunk(j, carry):
        off = pl.multiple_of(j * FFN_CHUNK, FFN_CHUNK)
        gate = jnp.dot(h, wi_ref[:, pl.ds(off, FFN_CHUNK)], preferred_element_type=F32)
        up = jnp.dot(h, wi_ref[:, pl.ds(hidden + off, FFN_CHUNK)], preferred_element_type=F32)
        act = (_silu(gate) * up).astype(BF16)
        out_ref[...] += jnp.dot(act, wd_ref[pl.ds(off, FFN_CHUNK), :], preferred_element_type=F32)
        return carry

    lax.fori_loop(0, hidden // FFN_CHUNK, hidden_chunk, 0)


def post_ffn(x, m, wo, g, w_in, w_out, tm, name):
    n, d = x.shape
    resident = lambda a: pl.BlockSpec(a.shape, lambda i: (0,) * a.ndim, pipeline_mode=pl.Buffered(1))
    return pl.pallas_call(
        _post_ffn_body,
        grid=(n // tm,),
        in_specs=[pl.BlockSpec((tm, d), lambda i: (i, 0)),
                  pl.BlockSpec((tm, m.shape[1]), lambda i: (i, 0)),
                  resident(wo),
                  pl.BlockSpec((1, d), lambda i: (0, 0)),
                  resident(w_in),
                  resident(w_out)],
        out_specs=pl.BlockSpec((tm, d), lambda i: (i, 0)),
        out_shape=jax.ShapeDtypeStruct((n, d), F32),
        compiler_params=_params(1),
        name=name,
    )(x, m, wo, g, w_in, w_out)


def _head_pair_norm(y, gain, lo):
    y2 = y * y
    s_lo = jnp.sum(jnp.where(lo, y2, 0.0), axis=-1, keepdims=True)
    s_hi = jnp.sum(jnp.where(lo, 0.0, y2), axis=-1, keepdims=True)
    ms = jnp.where(lo, s_lo, s_hi) * (1.0 / HEAD_DIM)
    return y * lax.rsqrt(ms + NORM_EPS) * gain


def _top_blocks(gate, axis):
    nb = gate.shape[axis]
    idx = lax.broadcasted_iota(jnp.int32, gate.shape, axis)
    rank = jnp.zeros(gate.shape, jnp.int32)
    for m in range(nb):
        c = lax.slice_in_dim(gate, m, m + 1, axis=axis)
        beats = (c > gate) | ((c == gate) & (idx > m))
        rank = rank + beats.astype(jnp.int32)
    return rank < MOBA_TOPK


def _moba_prompt_body(q_ref, k_ref, v_ref, gq_ref, gk_ref, *rest, n_prev):
    if n_prev:
        pk_ref, pv_ref, o_ref, kh_ref, vh_ref, kb_ref, vt_ref, qc_ref, s_ref = rest
    else:
        o_ref, kh_ref, vh_ref, kb_ref, vt_ref, qc_ref, s_ref = rest
    nb = k_ref.shape[0] // MOBA_BLOCK
    tq = MOBA_BLOCK
    lo = lax.broadcasted_iota(jnp.int32, (1, LANES), 1) < HEAD_DIM

    kn = _head_pair_norm(k_ref[...], gk_ref[...], lo)
    qn = _head_pair_norm(q_ref[...], gq_ref[...], lo)
    kt = kn.T
    vt = v_ref[...].T
    qt = (qn * (HEAD_DIM ** -0.5 * LOG2E)).T.astype(BF16)
    kh_ref[n_prev] = kt
    vh_ref[n_prev] = vt
    if n_prev:
        kh_ref[:n_prev] = pk_ref[...]
        vh_ref[:n_prev] = pv_ref[...]
    kmb = (jnp.sum(kn.reshape(nb, MOBA_BLOCK, LANES), axis=1) * (1.0 / MOBA_BLOCK)).astype(BF16)
    dim = lax.broadcasted_iota(jnp.int32, (LANES, 1), 0)
    zero = jnp.zeros_like(qt)
    q_lo, q_hi = jnp.where(dim < HEAD_DIM, qt, zero), jnp.where(dim >= HEAD_DIM, qt, zero)
    for n in range(nb):
        span = slice(n * MOBA_BLOCK, (n + 1) * MOBA_BLOCK)
        kb_ref[n] = kn[span].astype(BF16)
        vt_ref[n] = vt[:, span].astype(BF16)
        qc_ref[n] = jnp.concatenate([q_lo[:, span], q_hi[:, span]], axis=1)

    blk = lax.broadcasted_iota(jnp.int32, (nb, 2 * tq), 0)
    key = lax.broadcasted_iota(jnp.int32, (MOBA_BLOCK, 2 * tq), 0)
    qry = jnp.bitwise_and(lax.broadcasted_iota(jnp.int32, (MOBA_BLOCK, 2 * tq), 1), tq - 1)
    causal = key <= qry

    for i in range(nb):
        qcat = qc_ref[i]
        s_own = jnp.dot(kb_ref[i], qcat, preferred_element_type=F32)
        s_own = jnp.where(causal, s_own, NEG_INF)
        m = jnp.max(s_own, axis=0, keepdims=True)
        if i:
            past = blk < i
            gate = jnp.dot(kmb, qcat, preferred_element_type=F32)
            gate = jnp.where(past, gate, NEG_INF)
            sel = jnp.logical_and(_top_blocks(gate, 0), past).astype(F32)
        scores = s_ref.at[i % 2]
        for n in range(i):
            rows = slice(n * MOBA_BLOCK, (n + 1) * MOBA_BLOCK)
            s = jnp.dot(kb_ref[n], qcat, preferred_element_type=F32)
            s = jnp.where(sel[n:n + 1, :] > 0.5, s, NEG_INF)
            scores[rows, :] = s
            m = jnp.maximum(m, jnp.max(s, axis=0, keepdims=True))
        p = jnp.exp2(s_own - m)
        l = jnp.sum(p, axis=0, keepdims=True)
        acc = jnp.dot(vt_ref[i], p.astype(BF16), preferred_element_type=F32)
        for n in range(i):
            rows = slice(n * MOBA_BLOCK, (n + 1) * MOBA_BLOCK)
            p = jnp.exp2(scores[rows, :] - m)
            l = l + jnp.sum(p, axis=0, keepdims=True)
            acc = acc + jnp.dot(vt_ref[n], p.astype(BF16), preferred_element_type=F32)
        o = acc / l
        o_ref[i * tq:(i + 1) * tq, :] = jnp.concatenate([o[:HEAD_DIM, :tq], o[HEAD_DIM:, tq:]],
                                                        axis=0).T.astype(o_ref.dtype)


def moba_prompt(q, k, v, gq, gk, batch, seq, name, prev_k=None, prev_v=None):
    n = batch * seq
    nb = seq // MOBA_BLOCK
    n_prev = 0 if prev_k is None else prev_k.shape[0]
    rows = pl.BlockSpec((seq, LANES), lambda b, h: (b, h))
    gain = pl.BlockSpec((1, LANES), lambda b, h: (0, 0))
    hist = lambda layers: pl.BlockSpec((layers, None, LANES, seq), lambda b, h: (0, b, h, 0))
    in_specs = [rows, rows, rows, gain, gain]
    hist_shape = jax.ShapeDtypeStruct((n_prev + 1, batch, D_MODEL, seq), F32)
    out_specs = [rows, hist(n_prev + 1), hist(n_prev + 1)]
    out_shape = [jax.ShapeDtypeStruct((n, D_MODEL), BF16), hist_shape, hist_shape]
    args = [q, k, v, gq, gk]
    if n_prev:
        in_specs += [hist(n_prev)] * 2
        args += [prev_k, prev_v]
    return pl.pallas_call(
        functools.partial(_moba_prompt_body, n_prev=n_prev),
        grid=(batch, N_SLABS),
        in_specs=in_specs,
        out_specs=out_specs,
        out_shape=out_shape,
        scratch_shapes=[pltpu.VMEM((nb, MOBA_BLOCK, LANES), BF16),
                        pltpu.VMEM((nb, LANES, MOBA_BLOCK), BF16),
                        pltpu.VMEM((nb, LANES, HEADS_PER_SLAB * MOBA_BLOCK), BF16),
                        pltpu.VMEM((2, (nb - 1) * MOBA_BLOCK, HEADS_PER_SLAB * MOBA_BLOCK), F32)],
        compiler_params=_params(2),
        name=name,
    )(*args)


def _moba_sample_body(pt_ref, q_ref, k_ref, v_ref, gq_ref, gk_ref, hm_ref, hsel_ref, *rest, n_pages, n_prev):
    del pt_ref
    k_pages, v_pages = rest[:n_pages], rest[n_pages:2 * n_pages]
    if n_prev:
        pk_ref, pv_ref, o_ref, kh_ref, vh_ref, s_ref = rest[2 * n_pages:]
        kh_ref[:n_prev] = pk_ref[...]
        vh_ref[:n_prev] = pv_ref[...]
        vh_ref[n_prev] = v_ref[...]
    else:
        o_ref, kh_ref, s_ref = rest[2 * n_pages:]
    ppb = MOBA_BLOCK // PAGE_SIZE
    nb = n_pages // ppb
    s_new = q_ref.shape[0]
    rows = ATTN_HEADS * s_new
    lo = lax.broadcasted_iota(jnp.int32, (1, LANES), 1) < HEAD_DIM

    def heads_norm(y, gain):
        slabs = [_head_pair_norm(y[:, c * LANES:(c + 1) * LANES], gain[:, c * LANES:(c + 1) * LANES], lo)
                 for c in range(N_SLABS)]
        return jnp.concatenate(slabs, axis=1)

    qn = heads_norm(q_ref[...], gq_ref[...])
    kn = heads_norm(k_ref[...], gk_ref[...])
    kh_ref[n_prev] = kn
    qs = qn * (HEAD_DIM ** -0.5)
    gh, gw = hm_ref.shape[0] // s_new, hm_ref.shape[1]
    gr = gh * s_new
    n_groups = ATTN_HEADS // gh
    hm = hm_ref[...]
    q_groups = [(jnp.broadcast_to(qs[None, :, g * gw:(g + 1) * gw], (gh, s_new, gw)).reshape(gr, gw) * hm).astype(BF16)
                for g in range(n_groups)]

    for p in range(n_pages):
        for g in range(n_groups):
            s_ref[g * gr:(g + 1) * gr, p * PAGE_SIZE:(p + 1) * PAGE_SIZE] = jnp.dot(
                q_groups[g], k_pages[p][g * gw:(g + 1) * gw, :].astype(BF16), preferred_element_type=F32)

    def block_scores(n):
        return s_ref[:, n * MOBA_BLOCK:(n + 1) * MOBA_BLOCK]

    blk = lax.broadcasted_iota(jnp.int32, (rows, nb), 1)
    gate = jnp.zeros((rows, nb), F32)
    for n in range(nb):
        g_n = jnp.sum(block_scores(n), axis=-1, keepdims=True) * (1.0 / MOBA_BLOCK)
        gate = jnp.where(blk == n, g_n, gate)
    sel = _top_blocks(gate, 1)

    knb = kn.astype(BF16)
    s_own = jnp.concatenate(
        [lax.dot_general(q_groups[g], knb[:, g * gw:(g + 1) * gw], _NT, preferred_element_type=F32)
         for g in range(n_groups)], axis=0)
    q_pos = jnp.bitwise_and(lax.broadcasted_iota(jnp.int32, (rows, s_new), 0), s_new - 1)
    k_pos = lax.broadcasted_iota(jnp.int32, (rows, s_new), 1)
    s_own = jnp.where(k_pos <= q_pos, s_own, NEG_INF)

    def masked(n):
        return jnp.where(sel[:, n:n + 1], block_scores(n), NEG_INF)

    m = jnp.max(s_own, axis=-1, keepdims=True)
    for n in range(nb):
        m = jnp.maximum(m, jnp.max(masked(n), axis=-1, keepdims=True))
    p_own = jnp.exp(s_own - m)
    l = jnp.sum(p_own, axis=-1, keepdims=True)
    pob, vb = p_own.astype(BF16), v_ref[...].astype(BF16)
    accs = [jnp.dot(pob[g * gr:(g + 1) * gr], vb[:, g * gw:(g + 1) * gw], preferred_element_type=F32)
            for g in range(n_groups)]
    for n in range(nb):
        p = jnp.exp(masked(n) - m)
        l = l + jnp.sum(p, axis=-1, keepdims=True)
        pb = p.astype(BF16)
        for c in range(ppb):
            page = v_pages[n * ppb + c]
            for g in range(n_groups):
                accs[g] = accs[g] + lax.dot_general(
                    pb[g * gr:(g + 1) * gr, c * PAGE_SIZE:(c + 1) * PAGE_SIZE],
                    page[g * gw:(g + 1) * gw, :].astype(BF16), _NT, preferred_element_type=F32)
    hsel = hsel_ref[...]
    outs = []
    for g in range(n_groups):
        o = accs[g] / l[g * gr:(g + 1) * gr]
        outs.append(jnp.sum(o.reshape(gh, s_new, gw) * hsel[:, None, :], axis=0))
    o_ref[...] = jnp.concatenate(outs, axis=1)


def moba_sample(q, k, v, gq, gk, cache_kt, cache_vt, layer, page_table, name, prev_k=None, prev_v=None):
    dec_batch, n_pages = page_table.shape
    n_prev = 0 if prev_k is None else prev_k.shape[0]
    n = q.shape[0]
    s_new = n // dec_batch
    assert s_new & (s_new - 1) == 0, "new-token count must be a power of two"
    rows = ATTN_HEADS * s_new
    gh = SAMPLE_GROUP_HEADS
    gw = gh * HEAD_DIM
    head_of_lane = jnp.arange(gw) // HEAD_DIM
    hm = (jnp.arange(gh * s_new)[:, None] // s_new == head_of_lane[None, :]).astype(F32)
    hsel = (jnp.arange(gh)[:, None] == head_of_lane[None, :]).astype(F32)

    def page_spec(p):
        return pl.BlockSpec((None, None, D_MODEL, PAGE_SIZE),
                            lambda b, pt: (layer, pt[b * n_pages + p], 0, 0))

    const = lambda b, pt: (0, 0)
    token = pl.BlockSpec((s_new, D_MODEL), lambda b, pt: (b, 0))
    hist = lambda layers: pl.BlockSpec((layers, s_new, D_MODEL), lambda b, pt: (0, b, 0))
    in_specs = ([token, token, token,
                 pl.BlockSpec((1, D_MODEL), const),
                 pl.BlockSpec((1, D_MODEL), const),
                 pl.BlockSpec(hm.shape, const),
                 pl.BlockSpec(hsel.shape, const)]
                + [page_spec(p) for p in range(n_pages)] * 2)
    out_specs = [token, hist(n_prev + 1)]
    out_shape = [jax.ShapeDtypeStruct((n, D_MODEL), F32), jax.ShapeDtypeStruct((n_prev + 1, n, D_MODEL), F32)]
    args = [page_table.reshape(-1), q, k, v, gq, gk, hm, hsel, *([cache_kt] * n_pages), *([cache_vt] * n_pages)]
    if n_prev:
        in_specs += [hist(n_prev)] * 2
        out_specs.append(out_specs[-1])
        out_shape.append(out_shape[-1])
        args += [prev_k, prev_v]
    grid_spec = pltpu.PrefetchScalarGridSpec(
        num_scalar_prefetch=1,
        grid=(dec_batch,),
        in_specs=in_specs,
        out_specs=out_specs,
        scratch_shapes=[pltpu.VMEM((rows, n_pages * PAGE_SIZE), F32)],
    )
    return pl.pallas_call(
        functools.partial(_moba_sample_body, n_pages=n_pages, n_prev=n_prev),
        grid_spec=grid_spec,
        out_shape=out_shape,
        compiler_params=_params(1),
        name=name,
    )(*args)


def _cumsum_rows(tri, a):
    hi = a.astype(BF16)
    r1 = a - hi.astype(F32)
    mid = r1.astype(BF16)
    lo = (r1 - mid.astype(F32)).astype(BF16)
    n = a.shape[1]
    r = jnp.dot(tri, jnp.concatenate([hi, mid, lo], axis=1), preferred_element_type=F32)
    return r[:, :n] + r[:, n:2 * n] + r[:, 2 * n:]


def _boundary_rows(b, w):
    c, d = b.shape
    half = w // 2
    if half >= SUBLANES:
        return jnp.concatenate([jnp.broadcast_to(b[j * w + half - 1:j * w + half], (w, d)) for j in range(c // w)], axis=0)
    b3 = b.reshape(c // SUBLANES, SUBLANES, d)
    sub = lax.broadcasted_iota(jnp.int32, (1, SUBLANES, 1), 1)
    picks = [b3[:, j * w + half - 1:j * w + half, :] for j in range(SUBLANES // w)]
    ref = picks[-1]
    for j in range(SUBLANES // w - 2, -1, -1):
        ref = jnp.where(sub < (j + 1) * w, picks[j], ref)
    return jnp.broadcast_to(ref, b3.shape).reshape(c, d)


def _gla_intra(q, k, b, c):
    heads = [slice(h * GLA_DK, (h + 1) * GLA_DK) for h in range(q.shape[1] // GLA_DK)]
    row = lax.broadcasted_iota(jnp.int32, (c, 1), 0)
    rr = lax.broadcasted_iota(jnp.int32, (c, c), 0)
    cc = lax.broadcasted_iota(jnp.int32, (c, c), 1)
    apart = jnp.bitwise_xor(rr, cc)
    qk = q * k
    a = [jnp.where(rr == cc, jnp.sum(qk[:, h], axis=-1, keepdims=True), 0.0) for h in heads]
    w = c
    while w >= 2:
        d = b - _boundary_rows(b, w)
        f = jnp.exp(jnp.minimum(d, -d))
        lower = jnp.bitwise_and(row, w // 2) != 0
        qt = jnp.where(lower, q * f, 0.0).astype(BF16)
        kt = jnp.where(lower, 0.0, k * f).astype(BF16)
        for i, h in enumerate(heads):
            a_w = lax.dot_general(qt[:, h], kt[:, h], _NT, preferred_element_type=F32)
            a[i] = a[i] + (a_w if w == c else jnp.where(apart < w, a_w, 0.0))
        w //= 2
    return a


def _gla_body(q_ref, k_ref, v_ref, r_ref, gl_ref, wg_ref, bg_ref, tri_ref, gn_ref, s0_ref, *rest,
              c, chunks_per_seq, n_prev):
    if n_prev:
        ps_ref, o_ref, so_ref, s_ref = rest
    else:
        o_ref, so_ref, s_ref = rest
    t = pl.program_id(1)

    @pl.when(t == 0)
    def _():
        s_ref[...] = s0_ref[...]

    tri = tri_ref[...]
    scale = GLA_DK ** -0.5

    n_seqs = s0_ref.shape[0]

    def chunk(ci, carry):
        rows = pl.ds(pl.multiple_of(ci * c, c), c)
        u = ci if chunks_per_seq == 1 else lax.div(ci, chunks_per_seq)
        x = jnp.dot(gl_ref[rows, :].astype(BF16), wg_ref[...], preferred_element_type=F32) + bg_ref[...]
        log_a = (jnp.minimum(x, 0.0) - jnp.log1p(jnp.exp(-jnp.abs(x)))) * (1.0 / GLA_GATE_TAU)
        b = _cumsum_rows(tri, log_a)
        q = q_ref[rows, :] * scale
        k = k_ref[rows, :]
        v = v_ref[rows, :].astype(BF16)
        b_last = b[c - 1:c]
        q_in = (q * jnp.exp(b)).astype(BF16)
        k_out = (k * jnp.exp(b_last - b)).astype(BF16)
        decay = jnp.broadcast_to(jnp.exp(b_last), (SUBLANES, b.shape[1])).T[:, :1]
        a = _gla_intra(q, k, b, c)
        outs = []
        for hd in range(GLA_HEADS):
            kc = slice(hd * GLA_DK, (hd + 1) * GLA_DK)
            vc = slice(hd * GLA_DV, (hd + 1) * GLA_DV)
            state = s_ref[u, hd]
            o = jnp.dot(q_in[:, kc], state.astype(BF16), preferred_element_type=F32)
            o = o + jnp.dot(a[hd].astype(BF16), v[:, vc], preferred_element_type=F32)
            s_ref[u, hd] = state * decay[kc] + lax.dot_general(k_out[:, kc], v[:, vc], _TN, preferred_element_type=F32)
            outs.append(o * _rms_scale(o))
        o_ref[rows, :] = (jnp.concatenate(outs, axis=1) * gn_ref[...] * _silu(r_ref[rows, :])).astype(o_ref.dtype)
        return carry

    lax.fori_loop(0, n_seqs * chunks_per_seq, chunk, 0, unroll=True)

    @pl.when(t == pl.num_programs(1) - 1)
    def _():
        if n_prev:
            so_ref[:n_prev] = ps_ref[...]
        so_ref[n_prev] = s_ref[...]


def gla_mix(proj, w_gate, b_gate, g_out, s0, layer, batch, seq, c, tt, name, prev_states=None, out_dtype=F32,
            seqs=1):
    n = batch * seq
    nt = seq // tt
    assert seqs == 1 or nt == 1
    hk, hv = GLA_HEADS * GLA_DK, GLA_HEADS * GLA_DV
    n_prev = 0 if prev_states is None else prev_states.shape[0]
    tri = jnp.tril(jnp.ones((c, c), F32)).astype(BF16)
    row = lambda b, t: b * nt + t
    state_block = (None, seqs, GLA_HEADS, GLA_DK, GLA_DV)
    hist = lambda layers: pl.BlockSpec((layers, seqs, GLA_HEADS, GLA_DK, GLA_DV), lambda b, t: (0, b, 0, 0, 0))
    tr = seqs * tt
    in_specs = [pl.BlockSpec((tr, hk), lambda b, t: (row(b, t), 0)),
                pl.BlockSpec((tr, hk), lambda b, t: (row(b, t), 1)),
                pl.BlockSpec((tr, hv), lambda b, t: (row(b, t), 1)),
                pl.BlockSpec((tr, hv), lambda b, t: (row(b, t), 2)),
                pl.BlockSpec((tr, LANES), lambda b, t: (row(b, t), (2 * hk + 2 * hv) // LANES)),
                pl.BlockSpec(w_gate.shape, lambda b, t: (0, 0)),
                pl.BlockSpec((1, hk), lambda b, t: (0, 0)),
                pl.BlockSpec((c, c), lambda b, t: (0, 0)),
                pl.BlockSpec((1, hv), lambda b, t: (0, 0)),
                pl.BlockSpec(state_block, lambda b, t: (layer, b, 0, 0, 0))]
    args = [proj, proj, proj, proj, proj, w_gate, b_gate, tri, g_out, s0]
    if n_prev:
        in_specs.append(hist(n_prev))
        args.append(prev_states)
    return pl.pallas_call(
        functools.partial(_gla_body, c=c, chunks_per_seq=tt // c, n_prev=n_prev),
        grid=(batch // seqs, nt),
        in_specs=in_specs,
        out_specs=[pl.BlockSpec((tr, hv), lambda b, t: (row(b, t), 0)), hist(n_prev + 1)],
        out_shape=[jax.ShapeDtypeStruct((n, hv), out_dtype),
                   jax.ShapeDtypeStruct((n_prev + 1, batch, GLA_HEADS, GLA_DK, GLA_DV), F32)],
        scratch_shapes=[pltpu.VMEM((seqs, GLA_HEADS, GLA_DK, GLA_DV), F32)],
        compiler_params=_params(2),
        name=name,
    )(*args)


def kernel(x_prompt, x_sample, cache_k, cache_v, state_gla, page_table, norm_mixer, norm_ffn, w_qkv, q_norm,
           k_norm, w_attn_o, w_gla_in, w_gla_gate, b_gla_gate, gla_norm, w_gla_o, w_ffn_in, w_ffn_out):
    bp, tp, d = x_prompt.shape
    bs, ts, _ = x_sample.shape
    xp = x_prompt.reshape(bp * tp, d)
    xs = x_sample.reshape(bs * ts, d)
    n_layers_attn, n_pool = cache_k.shape[:2]
    ckt = jnp.transpose(cache_k, (0, 1, 3, 4, 2)).reshape(n_layers_attn, n_pool, d, PAGE_SIZE)
    cvt = jnp.transpose(cache_v, (0, 1, 3, 4, 2)).reshape(n_layers_attn, n_pool, d, PAGE_SIZE)
    hk, hv = GLA_HEADS * GLA_DK, GLA_HEADS * GLA_DV
    zero_state = jnp.zeros((1, bp, GLA_HEADS, GLA_DK, GLA_DV), F32)
    tm_p, tm_s = 1024, 512
    head_shape = (ATTN_HEADS, HEAD_DIM)

    kh_p = vh_p = kh_s = vh_s = st_p = st_s = None
    for i in range(DEPTH):
        j = i // 2
        g_mix = norm_mixer[i][None]
        if i % 2 == 0:
            w = w_qkv[j].astype(BF16)
            q_p, k_p, v_p = norm_matmul(xp, g_mix, w, tm_p, [d, d, d], f"qkv_p{i}")
            q_s, k_s, v_s = norm_matmul(xs, g_mix, w, tm_s, [d, d, d], f"qkv_s{i}")
            gq = jnp.tile(q_norm[j], ATTN_HEADS)[None]
            gk = jnp.tile(k_norm[j], ATTN_HEADS)[None]
            mp, kh_p, vh_p = moba_prompt(q_p, k_p, v_p, gq[:, :LANES], gk[:, :LANES], bp, tp, f"moba_p{i}",
                                         kh_p, vh_p)
            if kh_s is None:
                ms, kh_s = moba_sample(q_s, k_s, v_s, gq, gk, ckt, cvt, j, page_table, f"moba_s{i}")
                vh_s = v_s[None]
            else:
                ms, kh_s, vh_s = moba_sample(q_s, k_s, v_s, gq, gk, ckt, cvt, j, page_table, f"moba_s{i}",
                                             kh_s, vh_s)
            wo = w_attn_o[j].astype(BF16)
        else:
            w_main = w_gla_in[j][:, :2 * hk + 2 * hv]
            w_low = jnp.pad(w_gla_in[j][:, 2 * hk + 2 * hv:], ((0, 0), (0, LANES - GLA_GATE_RANK)))
            w = jnp.concatenate([w_main, w_low], axis=1).astype(BF16)
            proj_p, = norm_matmul(xp, g_mix, w, tm_p, [w.shape[1]], f"gla_in_p{i}")
            proj_s, = norm_matmul(xs, g_mix, w, tm_s, [w.shape[1]], f"gla_in_s{i}")
            w_gate = jnp.pad(w_gla_gate[j], ((0, LANES - GLA_GATE_RANK), (0, 0))).astype(BF16)
            b_gate = b_gla_gate[j][None]
            g_out = jnp.tile(gla_norm[j], GLA_HEADS)[None]
            mp, st_p = gla_mix(proj_p, w_gate, b_gate, g_out, zero_state, 0, bp, tp,
                               GLA_CHUNK, 512, f"gla_p{i}", st_p, BF16)
            ms, st_s = gla_mix(proj_s, w_gate, b_gate, g_out, state_gla, j, bs, ts, ts, ts, f"gla_s{i}", st_s,
                               seqs=GLA_SAMPLE_SEQS)
            wo = w_gla_o[j].astype(BF16)
        g_ffn = norm_ffn[i][None]
        w_in = w_ffn_in[i].astype(BF16)
        w_out = w_ffn_out[i].astype(BF16)
        xp = post_ffn(xp, mp, wo, g_ffn, w_in, w_out, tm_p, f"ffn_p{i}")
        xs = post_ffn(xs, ms, wo, g_ffn, w_in, w_out, tm_s, f"ffn_s{i}")
    rows_p = lambda h: jnp.transpose(h.reshape(h.shape[0], bp, *head_shape, tp), (0, 1, 4, 2, 3))
    rows_s = lambda h: h.reshape(h.shape[0], bs, ts, *head_shape)
    return (xp.reshape(bp, tp, d), xs.reshape(bs, ts, d), rows_p(kh_p), rows_p(vh_p), rows_s(kh_s), rows_s(vh_s),
            st_p, st_s)
```
